```python
import jax
import jax.numpy as jnp
from jax import lax
import numpy as np

D_MODEL = 2048
BATCH = 4
SEQ = 2048
DEPTH = 4

GRID_W = 64
CTX_LEN = 256

NA_HEADS = 8
NA_HEAD_DIM = 64
NA_WIDTH = NA_HEADS * NA_HEAD_DIM
NA_WIN_H = 8
NA_WIN_W = 16

RW_HEADS = 8
RW_HEAD_DIM = 64
RW_WIDTH = RW_HEADS * RW_HEAD_DIM
RW_LORA = 64
RW_GATE_LORA = 128
RW_GN_EPS = 64e-5

FN_GROUPS = 4
FN_GROUP_DIM = 128
FN_WIDTH = FN_GROUPS * FN_GROUP_DIM

SC_WIDTH = 512
SC_CONV = 3

BRANCH_WIDTH = 512
N_BRANCH = 4

D_FF = 5632
N_EXPERTS = 8
TOP_K = 2
D_FF_EXPERT = 5632

RMS_EPS = 1e-6
NEG_INF = -1e30

COL_LAYOUT = (
    ("na_k", NA_WIDTH), ("na_v", NA_WIDTH),
    ("rw_k", RW_WIDTH), ("rw_v", RW_WIDTH),
    ("rw_wlora", 2 * RW_LORA), ("rw_alora", 2 * RW_LORA),
    ("na_q", NA_WIDTH), ("rw_r", RW_WIDTH), ("rw_glora", RW_GATE_LORA),
    ("fn_x", FN_WIDTH),
    ("sc_b", SC_WIDTH), ("sc_c", SC_WIDTH), ("sc_x", SC_WIDTH),
    ("gates", N_BRANCH * D_MODEL),
)
N_STATE_SEGS = 6
CTX_STATE_COLS = sum(w for _, w in COL_LAYOUT[:N_STATE_SEGS])
W_IN_COLS = sum(w for _, w in COL_LAYOUT)

kernel_name = "hybrid_na_rwkv7_fnet_shortconv_moe_dit"


def _rmsnorm(x, g):
    xf = x.astype(jnp.float32)
    y = xf * lax.rsqrt(jnp.mean(xf * xf, axis=-1, keepdims=True) + RMS_EPS)
    return y.astype(x.dtype) * g


def _heads(z, n_heads):
    return z.reshape(*z.shape[:-1], n_heads, z.shape[-1] // n_heads)


def _split_cols(u, n_segs):
    out = {}
    off = 0
    for name, w in COL_LAYOUT[:n_segs]:
        out[name] = u[..., off:off + w]
        off += w
    return out


def _neighbourhood_attention(q, k, v, k_ctx, v_ctx, rel_bias):
    Bn, T, H, Dh = q.shape
    R = T // GRID_W
    kh = min(NA_WIN_H, R)
    scale = Dh ** -0.5
    qg = q.reshape(Bn, R, GRID_W, H, Dh)
    kg = k.reshape(Bn, R, GRID_W, H, Dh)
    vg = v.reshape(Bn, R, GRID_W, H, Dh)
    rows = np.arange(R)
    row_start = np.clip(rows - kh // 2, 0, R - kh)
    row_idx = row_start[:, None] + np.arange(kh)[None, :]
    cols = np.arange(GRID_W)
    col_start = np.clip(cols - NA_WIN_W // 2, 0, GRID_W - NA_WIN_W)
    col_ok = (cols[None, :] >= col_start[:, None]) & (cols[None, :] < col_start[:, None] + NA_WIN_W)
    dr = row_idx - rows[:, None] + (NA_WIN_H - 1)
    dc = np.clip(cols[None, :] - cols[:, None] + (NA_WIN_W - 1), 0, 2 * NA_WIN_W - 2)
    k_rows = kg[:, row_idx]
    v_rows = vg[:, row_idx]
    bias = rel_bias[:, dr[:, None, :, None], dc[None, :, None, :]]
    s_nb = jnp.einsum("brqhd,brjkhd->bhrqjk", qg, k_rows).astype(jnp.float32) * scale
    s_nb = jnp.where(col_ok[:, None, :], s_nb + bias.astype(jnp.float32), NEG_INF)
    s_ctx = jnp.einsum("brqhd,bchd->bhrqc", qg, k_ctx).astype(jnp.float32) * scale
    n_nb = kh * GRID_W
    s = jnp.concatenate([s_nb.reshape(Bn, H, R, GRID_W, n_nb), s_ctx], axis=-1)
    p = jax.nn.softmax(s, axis=-1).astype(v.dtype)
    p_nb = p[..., :n_nb].reshape(Bn, H, R, GRID_W, kh, GRID_W)
    o = (jnp.einsum("bhrqjk,brjkhd->brqhd", p_nb, v_rows)
         + jnp.einsum("bhrqc,bchd->brqhd", p[..., n_nb:], v_ctx))
    return o.reshape(Bn, T, H * Dh)


def _context_attention(q, k, v):
    Bn, C, H, Dh = q.shape
    s = jnp.einsum("bqhd,bkhd->bhqk", q, k).astype(jnp.float32) * (Dh ** -0.5)
    p = jax.nn.softmax(s, axis=-1).astype(v.dtype)
    return jnp.einsum("bhqk,bkhd->bqhd", p, v).reshape(Bn, C, H * Dh)


def _to_scan_time(z):
    return jnp.stack([z, z[:, ::-1]])


def _dirs_to_scan_time(z):
    return jnp.stack([z[:, :, 0], z[:, ::-1, 1]])


def _shift_lerp(z, mu):
    prev = jnp.pad(z[:, :, :-1], ((0, 0), (0, 0), (1, 0), (0, 0)))
    return z + (prev - z) * mu[:, None, None, :]


def _l2norm_heads(z):
    zh = _heads(z.astype(jnp.float32), RW_HEADS)
    n = jnp.sqrt(jnp.sum(zh * zh, axis=-1, keepdims=True))
    return zh / jnp.maximum(n, 1e-12)


def _rwkv_prep(seg, mu_kvr, mu_lora, w0, w2, a0, a2, k_k, k_a, with_r):
    Bn, T, _ = seg["rw_k"].shape
    k = _shift_lerp(_to_scan_time(seg["rw_k"]), mu_kvr[:, 0])
    v = _shift_lerp(_to_scan_time(seg["rw_v"]), mu_kvr[:, 1])
    lw = _shift_lerp(_dirs_to_scan_time(seg["rw_wlora"].reshape(Bn, T, 2, RW_LORA)), mu_lora[:, 0])
    la = _shift_lerp(_dirs_to_scan_time(seg["rw_alora"].reshape(Bn, T, 2, RW_LORA)), mu_lora[:, 1])
    wz = (w0[:, None, None] + jnp.einsum("nbtr,nrc->nbtc", jnp.tanh(lw), w2)).astype(jnp.float32)
    decay = jnp.exp(-jnp.exp(-jax.nn.softplus(-wz) - 0.5))
    a = jax.nn.sigmoid((a0[:, None, None] + jnp.einsum("nbtr,nrc->nbtc", la, a2)).astype(jnp.float32))
    kk = _l2norm_heads(k * k_k)
    k = k.astype(jnp.float32) * (1.0 + (a - 1.0) * k_a.astype(jnp.float32))
    hd = lambda z: _heads(z.astype(jnp.float32), RW_HEADS)
    r = hd(_shift_lerp(_to_scan_time(seg["rw_r"]), mu_kvr[:, 2])) if with_r else None
    return (hd(decay), hd(k), hd(v), kk, hd(a), r)


def _rwkv_scan(S0, decay, k, v, kk, a, r):
    seqs = (decay, k, v, -kk, kk * a) + ((r,) if r is not None else ())
    seqs = tuple(jnp.moveaxis(s, 2, 0) for s in seqs)

    def step(S, inp):
        w_t, k_t, v_t, nkk_t, akk_t = inp[:5]
        S = (S * w_t[..., None, :]
             + jnp.einsum("...ij,...j->...i", S, nkk_t)[..., :, None] * akk_t[..., None, :]
             + v_t[..., :, None] * k_t[..., None, :])
        y = jnp.einsum("...ij,...j->...i", S, inp[5]) if len(inp) == 6 else None
        return S, y

    S, ys = lax.scan(step, S0, seqs)
    return S, (None if ys is None else jnp.moveaxis(ys, 0, 2))


def _rwkv_out(ys, r, k, v, glora, r_k, g2, ln_w, ln_b):
    Bn, T = glora.shape[:2]
    bonus = jnp.sum(r * k * _heads(r_k.astype(jnp.float32), RW_HEADS), axis=-1, keepdims=True) * v
    y = ys[0] + ys[1][:, ::-1]
    bonus = bonus[0] + bonus[1][:, ::-1]
    mu = jnp.mean(y, axis=-1, keepdims=True)
    var = jnp.mean(jnp.square(y - mu), axis=-1, keepdims=True)
    y = ((y - mu) * lax.rsqrt(var + RW_GN_EPS)).reshape(Bn, T, RW_WIDTH) * ln_w + ln_b
    y = y + bonus.reshape(Bn, T, RW_WIDTH)
    g = jax.nn.sigmoid(glora) @ g2
    return (y * g).astype(glora.dtype)


def _fourier_mix(z):
    Bn, T, _ = z.shape
    zg = z.astype(jnp.float32).reshape(Bn, T, FN_GROUPS, FN_GROUP_DIM)
    f = jnp.fft.fft2(zg, axes=(1, 3), norm="ortho").real
    return f.reshape(Bn, T, FN_WIDTH).astype(z.dtype)


def _short_conv_mix(b, c, xin, conv_w):
    z = c * xin
    zp = jnp.pad(z, ((0, 0), (1, 1), (0, 0)))
    conv = zp[:, :-2] * conv_w[0] + zp[:, 1:-1] * conv_w[1] + zp[:, 2:] * conv_w[2]
    return b * conv


def _merge(branches, gate_logits, w_branch, w_out):
    yb = jnp.stack(branches, axis=-2)
    proj = jnp.einsum("btnc,ncd->btnd", yb, w_branch)
    gates = jax.nn.sigmoid(gate_logits.reshape(proj.shape))
    return jnp.einsum("btnd,btnd->btd", gates, proj) @ w_out


def _mixer_layer(h, hc, ctx_out, w_in, rel_bias, mu_kvr, mu_lora, w0, w2, a0, a2, k_k, k_a, r_k,
                 g2, ln_w, ln_b, conv_w, w_branch, w_out):
    u = _split_cols(h @ w_in, len(COL_LAYOUT))
    if ctx_out:
        uc = _split_cols(hc @ w_in, len(COL_LAYOUT))
    else:
        uc = _split_cols(hc @ w_in[:, :CTX_STATE_COLS], N_STATE_SEGS)
    k_ctx = _heads(uc["na_k"], NA_HEADS)
    v_ctx = _heads(uc["na_v"], NA_HEADS)
    y_na = _neighbourhood_attention(_heads(u["na_q"], NA_HEADS), _heads(u["na_k"], NA_HEADS),
                                    _heads(u["na_v"], NA_HEADS), k_ctx, v_ctx, rel_bias)
    rw = (mu_kvr, mu_lora, w0, w2, a0, a2, k_k, k_a)
    pc = _rwkv_prep(uc, *rw, with_r=ctx_out)
    S0 = jnp.zeros((2, h.shape[0], RW_HEADS, RW_HEAD_DIM, RW_HEAD_DIM), jnp.float32)
    S_ctx, ys_c = _rwkv_scan(S0, *pc)
    pl = _rwkv_prep(u, *rw, with_r=True)
    _, ys_l = _rwkv_scan(S_ctx, *pl)
    y_rw = _rwkv_out(ys_l, pl[5], pl[1], pl[2], u["rw_glora"], r_k, g2, ln_w, ln_b)
    y_fn = _fourier_mix(u["fn_x"])
    y_sc = _short_conv_mix(u["sc_b"], u["sc_c"], u["sc_x"], conv_w)
    y = _merge([y_na, y_rw, y_fn, y_sc], u["gates"], w_branch, w_out)
    if not ctx_out:
        return y, None
    yc_na = _context_attention(_heads(uc["na_q"], NA_HEADS), k_ctx, v_ctx)
    yc_rw = _rwkv_out(ys_c, pc[5], pc[1], pc[2], uc["rw_glora"], r_k, g2, ln_w, ln_b)
    yc_fn = _fourier_mix(uc["fn_x"])
    yc_sc = _short_conv_mix(uc["sc_b"], uc["sc_c"], uc["sc_x"], conv_w)
    yc = _merge([yc_na, yc_rw, yc_fn, yc_sc], uc["gates"], w_branch, w_out)
    return y, yc


def _swiglu(x, w1, w3, w2):
    return (jax.nn.silu(x @ w1) * (x @ w3)) @ w2


def _moe(x, router, w1, w3, w2):
    logits = (x @ router).astype(jnp.float32)
    top_val, top_idx = lax.top_k(logits, TOP_K)
    top_p = jax.nn.softmax(top_val, axis=-1)
    combine = jnp.sum(jax.nn.one_hot(top_idx, N_EXPERTS, dtype=jnp.float32) * top_p[..., None], axis=-2)
    combine = combine.astype(x.dtype)
    out = jnp.zeros_like(x)
    for e in range(N_EXPERTS):
        out = out + combine[:, e, None] * _swiglu(x, w1[e], w3[e], w2[e])
    return out


def setup_inputs(seed: int = 0) -> dict:
    key = jax.random.key(seed)
    ks = jax.random.split(key, 40)
    f32 = jnp.float32
    D = D_MODEL
    n_dense = (DEPTH + 1) // 2
    n_moe = DEPTH // 2

    def nrm(i, shape, scale):
        return jax.random.normal(ks[i], shape, f32) * scale

    def unif(i, shape, lo, hi):
        return jax.random.uniform(ks[i], shape, f32, lo, hi)

    return {
        "x": nrm(0, (BATCH, SEQ, D), 1.0),
        "c": nrm(1, (BATCH, D), 1.0),
        "ctx": nrm(2, (BATCH, CTX_LEN, D), 1.0),
        "c_ctx": nrm(3, (D,), 1.0),
        "ada_w": nrm(4, (DEPTH, D, 6 * D), 0.5 * D ** -0.5),
        "ada_b": nrm(5, (DEPTH, 6 * D), 0.02),
        "norm_mix": 1.0 + nrm(6, (DEPTH, D), 0.02),
        "norm_ffn": 1.0 + nrm(7, (DEPTH, D), 0.02),
        "w_in": nrm(8, (DEPTH, D, W_IN_COLS), D ** -0.5),
        "na_rel_bias": nrm(9, (DEPTH, NA_HEADS, 2 * NA_WIN_H - 1, 2 * NA_WIN_W - 1), 0.1),
        "rw_mu_kvr": unif(10, (DEPTH, 2, 3, RW_WIDTH), 0.0, 1.0),
        "rw_mu_lora": unif(11, (DEPTH, 2, 2, RW_LORA), 0.0, 1.0),
        "rw_w0": unif(12, (DEPTH, 2, RW_WIDTH), -5.0, 0.0),
        "rw_w2": nrm(13, (DEPTH, 2, RW_LORA, RW_WIDTH), 0.1),
        "rw_a0": nrm(14, (DEPTH, 2, RW_WIDTH), 0.5),
        "rw_a2": nrm(15, (DEPTH, 2, RW_LORA, RW_WIDTH), RW_LORA ** -0.5),
        "rw_k_k": 0.85 + nrm(16, (DEPTH, RW_WIDTH), 0.05),
        "rw_k_a": 1.0 + nrm(17, (DEPTH, RW_WIDTH), 0.05),
        "rw_r_k": nrm(18, (DEPTH, RW_WIDTH), 0.1),
        "rw_g2": nrm(19, (DEPTH, RW_GATE_LORA, RW_WIDTH), RW_GATE_LORA ** -0.5),
        "rw_ln_w": 1.0 + nrm(20, (DEPTH, RW_WIDTH), 0.02),
        "rw_ln_b": nrm(21, (DEPTH, RW_WIDTH), 0.02),
        "sc_conv": nrm(22, (DEPTH, SC_CONV, SC_WIDTH), SC_CONV ** -0.5),
        "w_branch": nrm(23, (DEPTH, N_BRANCH, BRANCH_WIDTH, D), BRANCH_WIDTH ** -0.5),
        "w_out": nrm(24, (DEPTH, D, D), D ** -0.5),
        "ffn_w1": nrm(25, (n_dense, D, D_FF), D ** -0.5),
        "ffn_w3": nrm(26, (n_dense, D, D_FF), D ** -0.5),
        "ffn_w2": nrm(27, (n_dense, D_FF, D), D_FF ** -0.5),
        "moe_router": nrm(28, (n_moe, D, N_EXPERTS), D ** -0.5),
        "moe_w1": nrm(29, (n_moe, N_EXPERTS, D, D_FF_EXPERT), D ** -0.5),
        "moe_w3": nrm(30, (n_moe, N_EXPERTS, D, D_FF_EXPERT), D ** -0.5),
        "moe_w2": nrm(31, (n_moe, N_EXPERTS, D_FF_EXPERT, D), D_FF_EXPERT ** -0.5),
        "norm_final": 1.0 + nrm(32, (D,), 0.02),
    }


def reference(x, c, ctx, c_ctx, ada_w, ada_b, norm_mix, norm_ffn, w_in, na_rel_bias,
              rw_mu_kvr, rw_mu_lora, rw_w0, rw_w2, rw_a0, rw_a2, rw_k_k, rw_k_a, rw_r_k, rw_g2,
              rw_ln_w, rw_ln_b, sc_conv, w_branch, w_out, ffn_w1, ffn_w3, ffn_w2,
              moe_router, moe_w1, moe_w3, moe_w2, norm_final):
    T = x.shape[1]
    xc = ctx
    sc_c = jax.nn.silu(c)
    sc_ctx = jax.nn.silu(c_ctx)
    for l in range(DEPTH):
        ctx_out = l < DEPTH - 1
        sh1, s1, g1, sh2, s2, g2 = jnp.split(sc_c @ ada_w[l] + ada_b[l], 6, axis=-1)
        n_mod = 6 if ctx_out else 2
        mod_c = jnp.split(sc_ctx @ ada_w[l][:, :n_mod * D_MODEL] + ada_b[l][:n_mod * D_MODEL], n_mod)

        h = _rmsnorm(x, norm_mix[l]) * (1.0 + s1[:, None]) + sh1[:, None]
        hc = _rmsnorm(xc, norm_mix[l]) * (1.0 + mod_c[1]) + mod_c[0]
        y, yc = _mixer_layer(h, hc, ctx_out, w_in[l], na_rel_bias[l], rw_mu_kvr[l], rw_mu_lora[l],
                             rw_w0[l], rw_w2[l], rw_a0[l], rw_a2[l], rw_k_k[l], rw_k_a[l], rw_r_k[l],
                             rw_g2[l], rw_ln_w[l], rw_ln_b[l], sc_conv[l], w_branch[l], w_out[l])
        x = x + g1[:, None] * y

        hf = _rmsnorm(x, norm_ffn[l]) * (1.0 + s2[:, None]) + sh2[:, None]
        if ctx_out:
            xc = xc + mod_c[2] * yc
            hfc = _rmsnorm(xc, norm_ffn[l]) * (1.0 + mod_c[4]) + mod_c[3]
            tok = jnp.concatenate([hf, hfc], axis=1)
        else:
            tok = hf
        flat = tok.reshape(-1, D_MODEL)
        if l % 2 == 0:
            i = l // 2
            f = _swiglu(flat, ffn_w1[i], ffn_w3[i], ffn_w2[i])
        else:
            i = l // 2
            f = _moe(flat, moe_router[i], moe_w1[i], moe_w3[i], moe_w2[i])
        f = f.reshape(tok.shape)
        x = x + g2[:, None] * f[:, :T]
        if ctx_out:
            xc = xc + mod_c[5] * f[:, T:]
    return _rmsnorm(x, norm_final)
```

```python
import functools

import jax
import jax.numpy as jnp
import numpy as np
from jax import lax
from jax.experimental import pallas as pl
from jax.experimental.pallas import tpu as pltpu

BF = jnp.bfloat16
F32 = jnp.float32
HI = lax.Precision.HIGHEST

D = 2048
NB = 4
T = 2048
CTX = 256
S = CTX + T
M = NB * S
DEPTH = 4
GRID_W = 64
ROWS = T // GRID_W
WIN_H = 8
WIN_W = 16
HEADS = 8
HD = 64
W = HEADS * HD
LORA = 64
D_FF = 5632
N_EXP = 8
RMS_EPS = 1e-6
GN_EPS = 64e-5
NEG = -1e30

U_NA_K, U_NA_V, U_RW_K, U_RW_V, U_LORA, U_NA_Q, U_RW_R, U_FN_X, U_SC_B, U_SC_C, U_SC_X, U_GATES = range(12)
U_COLS = 27 * 512
LORA_W, LORA_A, LORA_G = 0, 128, 256

CHUNK = 64
N_CHUNK = S // CHUNK
CTX_CHUNKS = CTX // CHUNK
TOK_TILE = 256
MOE_TM = 1024
MOE_TILES = (2 * M) // MOE_TM + N_EXP


def _params(sem, vmem_mb):
    return pltpu.CompilerParams(dimension_semantics=sem, vmem_limit_bytes=vmem_mb << 20)


def _dot(a, b, **kw):
    return jnp.dot(a, b, preferred_element_type=F32, **kw)


def _dot_nt(a, b):
    return lax.dot_general(a, b, (((1,), (1,)), ((), ())), preferred_element_type=F32)


def _dot_tn(a, b):
    return lax.dot_general(a, b, (((0,), (0,)), ((), ())), preferred_element_type=F32)


def _ada_kernel(c_ref, w_ref, b_ref, o_ref):
    c = c_ref[...]
    a = (c * jax.nn.sigmoid(c)).astype(BF)
    o_ref[0] = _dot(a, w_ref[0].astype(BF)) + b_ref[0]


def _ada(cvec, ada_w, ada_b):
    tn = 1024
    return pl.pallas_call(
        _ada_kernel,
        grid=(DEPTH, 6 * D // tn),
        in_specs=[
            pl.BlockSpec((8, D), lambda l, j: (0, 0)),
            pl.BlockSpec((1, D, tn), lambda l, j: (l, 0, j)),
            pl.BlockSpec((1, 1, tn), lambda l, j: (l, 0, j)),
        ],
        out_specs=pl.BlockSpec((1, 8, tn), lambda l, j: (l, 0, j)),
        out_shape=jax.ShapeDtypeStruct((DEPTH, 8, 6 * D), F32),
        compiler_params=_params(("arbitrary", "arbitrary"), 40),
        name="ada",
    )(cvec, ada_w, ada_b.reshape(DEPTH, 1, 6 * D))


def _rn_kernel(*refs, has_delta, has_router, final):
    it = iter(refs)
    x_ref = next(it)
    d_ref = next(it) if has_delta else None
    mod_ref = next(it)
    g_ref = next(it)
    r_ref = next(it) if has_router else None
    xo_ref = next(it) if (has_delta and not final) else None
    h_ref = next(it)
    lg_ref = next(it) if has_router else None
    x = x_ref[0]
    mod = mod_ref[0, 0]
    if has_delta:
        x = x + mod[0:1] * d_ref[0]
        if xo_ref is not None:
            xo_ref[0] = x
    y = x * lax.rsqrt(jnp.mean(x * x, axis=-1, keepdims=True) + RMS_EPS) * g_ref[...]
    if final:
        h_ref[0] = y
        return
    h = y * (1.0 + mod[1:2]) + mod[2:3]
    h_ref[0] = h.astype(BF)
    if has_router:
        lg_ref[0] = _dot(h, r_ref[...], precision=HI)


def _resid_norm(x, delta, mod, g, router=None, final=False):
    has_delta = delta is not None
    has_router = router is not None
    n_t = S // TOK_TILE
    off = CTX // TOK_TILE if final else 0
    grid = (NB, n_t - off)
    tok = pl.BlockSpec((1, TOK_TILE, D), lambda b, i: (b, i + off, 0))
    in_specs = [tok] + ([tok] if has_delta else [])
    in_specs += [
        pl.BlockSpec((1, 1, 8, D), lambda b, i: (b, jnp.minimum(i + off, 1), 0, 0)),
        pl.BlockSpec((1, D), lambda b, i: (0, 0)),
    ]
    args = [x] + ([delta] if has_delta else []) + [mod, g.reshape(1, D)]
    out_specs, out_shape = [], []
    if final:
        out_specs.append(pl.BlockSpec((1, TOK_TILE, D), lambda b, i: (b, i, 0)))
        out_shape.append(jax.ShapeDtypeStruct((NB, T, D), F32))
    else:
        if has_delta:
            out_specs.append(tok)
            out_shape.append(jax.ShapeDtypeStruct((NB, S, D), F32))
        out_specs.append(tok)
        out_shape.append(jax.ShapeDtypeStruct((NB, S, D), BF))
    if has_router:
        in_specs.append(pl.BlockSpec((D, 128), lambda b, i: (0, 0)))
        args.append(router)
        out_specs.append(pl.BlockSpec((1, TOK_TILE, 128), lambda b, i: (b, i, 0)))
        out_shape.append(jax.ShapeDtypeStruct((NB, S, 128), F32))
    return pl.pallas_call(
        functools.partial(_rn_kernel, has_delta=has_delta, has_router=has_router, final=final),
        grid=grid,
        in_specs=in_specs,
        out_specs=out_specs,
        out_shape=out_shape,
        compiler_params=_params(("arbitrary", "arbitrary"), 40),
        name="resid_norm",
    )(*args)


def _mm_kernel(a_ref, w_ref, o_ref):
    o_ref[...] = _dot(a_ref[...].astype(BF), w_ref[...].astype(BF)).astype(o_ref.dtype)


def _matmul(a, w, *, tm, tn, out_dtype, a_col_blk=0, k=None, name="matmul"):
    m = a.shape[0]
    k = k or a.shape[1]
    n = w.shape[1]
    return pl.pallas_call(
        _mm_kernel,
        grid=(n // tn, m // tm),
        in_specs=[
            pl.BlockSpec((tm, k), lambda j, i: (i, a_col_blk)),
            pl.BlockSpec((k, tn), lambda j, i: (0, j)),
        ],
        out_specs=pl.BlockSpec((tm, tn), lambda j, i: (i, j)),
        out_shape=jax.ShapeDtypeStruct((m, n), out_dtype),
        compiler_params=_params(("arbitrary", "arbitrary"), 48),
        name=name,
    )(a, w)


def _na_window(r):
    rr = r - CTX // GRID_W
    rs = jnp.clip(rr - WIN_H // 2, 0, ROWS - WIN_H)
    return rr, rs


def _na_kernel(q_ref, k_ref, v_ref, bias_ref, o_ref):
    _, rs = _na_window(pl.program_id(1))
    kstart = pl.multiple_of(CTX + GRID_W * rs, GRID_W)
    lo = lax.broadcasted_iota(jnp.int32, (GRID_W, 128), 1) < HD
    n_nb = WIN_H * GRID_W
    for p in range(HEADS // 2):
        cs = slice(128 * p, 128 * p + 128)
        q = q_ref[0, :, cs] * (HD ** -0.5)
        q2 = jnp.concatenate([jnp.where(lo, q, 0.0), jnp.where(lo, 0.0, q)], axis=0).astype(BF)
        kn = k_ref[0, pl.ds(kstart, n_nb), cs].astype(BF)
        kc = k_ref[0, 0:CTX, cs].astype(BF)
        s_nb = _dot_nt(q2, kn) + bias_ref[0, 128 * p:128 * p + 128, :]
        s_cx = _dot_nt(q2, kc)
        m = jnp.maximum(jnp.max(s_nb, axis=-1, keepdims=True), jnp.max(s_cx, axis=-1, keepdims=True))
        e_nb = jnp.exp(s_nb - m)
        e_cx = jnp.exp(s_cx - m)
        l = jnp.sum(e_nb, axis=-1, keepdims=True) + jnp.sum(e_cx, axis=-1, keepdims=True)
        vn = v_ref[0, pl.ds(kstart, n_nb), cs].astype(BF)
        vc = v_ref[0, 0:CTX, cs].astype(BF)
        o2 = (_dot(e_nb.astype(BF), vn) + _dot(e_cx.astype(BF), vc)) / l
        o_ref[0, :, cs] = jnp.where(lo, o2[0:GRID_W], o2[GRID_W:]).astype(o_ref.dtype)


def _na_bias_table(rel_bias):
    cols = np.arange(GRID_W)
    col_start = np.clip(cols - WIN_W // 2, 0, GRID_W - WIN_W)
    col_ok = (cols[None, :] >= col_start[:, None]) & (cols[None, :] < col_start[:, None] + WIN_W)
    dc = np.clip(cols[None, :] - cols[:, None] + (WIN_W - 1), 0, 2 * WIN_W - 2)
    d0 = np.arange(WIN_H)[:, None] + np.arange(WIN_H)[None, :]
    tab = rel_bias[:, d0[:, :, None, None], dc[None, None, :, :]]
    tab = jnp.where(col_ok[None, None, None], tab.astype(F32), NEG)
    tab = jnp.transpose(tab, (1, 0, 3, 2, 4)).reshape(WIN_H, HEADS * GRID_W, WIN_H * GRID_W)
    return jnp.concatenate([tab, jnp.full((1,) + tab.shape[1:], NEG, F32)], axis=0)


def _attention(u3, bias_tab):
    def bias_idx(b, r):
        rr, rs = _na_window(r)
        return (jnp.where(rr < 0, WIN_H, rs - rr + WIN_H - 1), 0, 0)

    return pl.pallas_call(
        _na_kernel,
        grid=(NB, S // GRID_W),
        in_specs=[
            pl.BlockSpec((1, GRID_W, W), lambda b, r: (b, r, U_NA_Q)),
            pl.BlockSpec((1, S, W), lambda b, r: (b, 0, U_NA_K)),
            pl.BlockSpec((1, S, W), lambda b, r: (b, 0, U_NA_V)),
            pl.BlockSpec((1, HEADS * GRID_W, WIN_H * GRID_W), bias_idx),
        ],
        out_specs=pl.BlockSpec((1, GRID_W, W), lambda b, r: (b, r, 0)),
        out_shape=jax.ShapeDtypeStruct((NB, S, W), BF),
        compiler_params=_params(("arbitrary", "arbitrary"), 48),
        name="attention",
    )(u3, u3, u3, bias_tab)


def _shift_prev(z, halo_row, first_is_start, rev):
    n = z.shape[0]
    row = lax.broadcasted_iota(jnp.int32, z.shape, 0)
    edge = jnp.where(first_is_start, 0.0, halo_row)
    if rev:
        return jnp.where(row == n - 1, edge, pltpu.roll(z, n - 1, 0))
    return jnp.where(row == 0, edge, pltpu.roll(z, 1, 0))


def _rw_prep_kernel(k_ref, v_ref, r_ref, l_ref, kh_ref, vh_ref, rh_ref, lh_ref, vec_ref, w2_ref, a2_ref,
                    ones_ref, lw_o, k_o, v_o, kk_o, b_o, r_o, bonus_o, *, rev):
    i = pl.program_id(1)
    n_t = S // TOK_TILE
    start = (i == n_t - 1) | (i == 0) if rev else (i <= 1)
    hrow = 0 if rev else 7
    vec = vec_ref[0]
    mu_k, mu_v, mu_r, mu_l, w0, a0, k_k, k_a, r_k = (vec[j:j + 1] for j in range(9))

    def lerp(ref, href, mu):
        z = ref[0]
        prev = _shift_prev(z, href[0, hrow:hrow + 1, :], start, rev)
        return z + (prev - z) * mu

    lz = lerp(l_ref, lh_ref, mu_l)
    wz = w0 + _dot(jnp.tanh(lz), w2_ref[0], precision=HI)
    a = jax.nn.sigmoid(a0 + _dot(lz, a2_ref[0], precision=HI))
    softplus = jnp.maximum(-wz, 0.0) + jnp.log1p(jnp.exp(-jnp.abs(wz)))
    lw_o[0] = -jnp.exp(-softplus - 0.5)
    k1 = lerp(k_ref, kh_ref, mu_k)
    kkr = k1 * k_k
    ones = ones_ref[...]
    nrm = jnp.sqrt(_dot(kkr * kkr, ones, precision=HI))
    kk = kkr / jnp.maximum(nrm, 1e-12)
    k2 = k1 * (1.0 + (a - 1.0) * k_a)
    v1 = lerp(v_ref, vh_ref, mu_v)
    r1 = lerp(r_ref, rh_ref, mu_r)
    k_o[0] = k2
    v_o[0] = v1
    kk_o[0] = kk
    b_o[0] = kk * a
    r_o[0] = r1
    bonus_o[0] = _dot(r1 * k2 * r_k, ones, precision=HI) * v1


def _rw_prep(u3, vec, w2e, a2e, ones_bd, d):
    rev = d == 1
    n_t = S // TOK_TILE
    hb = TOK_TILE // 8

    def tile(blk):
        return pl.BlockSpec((1, TOK_TILE, W), lambda b, i: (b, i, blk))

    def halo(blk):
        if rev:
            return pl.BlockSpec((1, 8, W), lambda b, i: (b, jnp.minimum((i + 1) * hb, S // 8 - 1), blk))
        return pl.BlockSpec((1, 8, W), lambda b, i: (b, jnp.maximum(i * hb - 1, 0), blk))

    blks = (U_RW_K, U_RW_V, U_RW_R, U_LORA)
    out = pl.BlockSpec((1, TOK_TILE, W), lambda b, i: (b, i, 0))
    return pl.pallas_call(
        functools.partial(_rw_prep_kernel, rev=rev),
        grid=(NB, n_t),
        in_specs=[tile(x) for x in blks] + [halo(x) for x in blks] + [
            pl.BlockSpec((1, 16, W), lambda b, i: (d, 0, 0)),
            pl.BlockSpec((1, W, W), lambda b, i: (d, 0, 0)),
            pl.BlockSpec((1, W, W), lambda b, i: (d, 0, 0)),
            pl.BlockSpec((W, W), lambda b, i: (0, 0)),
        ],
        out_specs=[out] * 7,
        out_shape=[jax.ShapeDtypeStruct((NB, S, W), F32)] * 7,
        compiler_params=_params(("arbitrary", "arbitrary"), 48),
        name="rw_prep",
    )(*([u3] * 8), vec, w2e, a2e, ones_bd)


def _pair_rows(z, lo):
    return jnp.concatenate([jnp.where(lo, z, 0.0), jnp.where(lo, 0.0, z)], axis=0)


def _rw_scan_kernel(lw_ref, k_ref, v_ref, kk_ref, b_ref, r_ref, y_ref, s_ref, *, rev):
    c = pl.program_id(1)

    @pl.when(c == 0)
    def _():
        s_ref[...] = jnp.zeros_like(s_ref)

    n = CHUNK
    t_i = lax.broadcasted_iota(jnp.int32, (n, n), 0)
    s_i = lax.broadcasted_iota(jnp.int32, (n, n), 1)
    incl1 = (s_i >= t_i) if rev else (s_i <= t_i)
    t2 = lax.broadcasted_iota(jnp.int32, (2 * n, 2 * n), 0)
    s2 = lax.broadcasted_iota(jnp.int32, (2 * n, 2 * n), 1)
    same = (t2 >= n) == (s2 >= n)
    tt, ss = t2 & (n - 1), s2 & (n - 1)
    incl2 = same & ((ss >= tt) if rev else (ss <= tt))
    strict2 = same & ((ss > tt) if rev else (ss < tt))
    eye2 = t2 == s2
    lo = lax.broadcasted_iota(jnp.int32, (n, 128), 1) < HD
    last = 0 if rev else n - 1

    lw_all = lw_ref[0]
    cum_all = _dot(incl1.astype(F32), lw_all, precision=HI)
    for p in range(HEADS // 2):
        cs = slice(128 * p, 128 * p + 128)
        lw, cum = lw_all[:, cs], cum_all[:, cs]
        tot = cum[last:last + 1, :]
        g, gp, gi, gc = jnp.exp(cum), jnp.exp(cum - lw), jnp.exp(-cum), jnp.exp(tot - cum)
        kk, b, k, r, v = kk_ref[0, :, cs], b_ref[0, :, cs], k_ref[0, :, cs], r_ref[0, :, cs], v_ref[0, :, cs]
        a2 = _pair_rows(-kk * gp, lo).astype(BF)
        r2 = _pair_rows(r * g, lo)
        b2 = _pair_rows(b * gi, lo).astype(BF)
        k2 = _pair_rows(k * gi, lo).astype(BF)
        bt2 = _pair_rows(b * gc, lo).astype(BF)
        kt2 = _pair_rows(k * gc, lo).astype(BF)
        v2 = _pair_rows(v, lo).astype(BF)
        sc = _dot_nt(jnp.concatenate([a2, r2.astype(BF)], axis=0), jnp.concatenate([b2, k2], axis=0))
        m_ab = jnp.where(strict2, sc[:2 * n, :2 * n], 0.0)
        m_ak = jnp.where(strict2, sc[:2 * n, 2 * n:], 0.0).astype(BF)
        m_rb = jnp.where(incl2, sc[2 * n:, :2 * n], 0.0).astype(BF)
        m_rk = jnp.where(incl2, sc[2 * n:, 2 * n:], 0.0).astype(BF)
        x = jnp.where(eye2, 1.0, 0.0) + m_ab
        pw = m_ab
        for _ in range(int(np.log2(n)) - 1):
            pwb = pw.astype(BF)
            pw = _dot(pwb, pwb)
            x = x + _dot(x.astype(BF), pw.astype(BF))
        xb = x.astype(BF)
        wm = _dot(xb, a2)
        u = _dot(xb, _dot(m_ak, v2).astype(BF))
        wmb, ub = wm.astype(BF), u.astype(BF)
        rm = r2 + _dot(m_rb, wmb)
        yu = _dot(m_rb, ub) + _dot(m_rk, v2)
        tm = jnp.where(eye2, jnp.exp(tot), 0.0) + _dot_tn(bt2, wmb)
        sv = _dot_tn(bt2, ub) + _dot_tn(kt2, v2)
        st = s_ref[p].astype(BF)
        y2 = _dot(rm.astype(BF), st) + yu
        y_ref[0, :, cs] = y2[:n] + y2[n:]
        s_ref[p] = _dot(tm.astype(BF), st) + sv


def _rw_scan(seqs, d):
    rev = d == 1

    def chunk_idx(b, c):
        if not rev:
            return (b, c, 0)
        return (b, jnp.where(c < CTX_CHUNKS, CTX_CHUNKS - 1 - c, N_CHUNK + CTX_CHUNKS - 1 - c), 0)

    spec = pl.BlockSpec((1, CHUNK, W), chunk_idx)
    return pl.pallas_call(
        functools.partial(_rw_scan_kernel, rev=rev),
        grid=(NB, N_CHUNK),
        in_specs=[spec] * 6,
        out_specs=spec,
        out_shape=jax.ShapeDtypeStruct((NB, S, W), F32),
        scratch_shapes=[pltpu.VMEM((HEADS // 2, 128, 128), F32)],
        compiler_params=_params(("arbitrary", "arbitrary"), 32),
        name="rw_scan",
    )(*seqs)


def _rw_out_kernel(y0_ref, y1_ref, b0_ref, b1_ref, l_ref, g2_ref, avg_ref, ln_ref, o_ref):
    y = y0_ref[0] + y1_ref[0]
    avg = avg_ref[...]
    mu = _dot(y, avg, precision=HI)
    yc = y - mu
    var = _dot(yc * yc, avg, precision=HI)
    ln = ln_ref[...]
    y = yc * lax.rsqrt(var + GN_EPS) * ln[0:1] + ln[1:2] + b0_ref[0] + b1_ref[0]
    gate = _dot(jax.nn.sigmoid(l_ref[0]).astype(BF), g2_ref[...])
    o_ref[0] = (y * gate).astype(o_ref.dtype)


def _rw_out(y0, y1, bon0, bon1, u3, g2e, avg_bd, ln):
    tile = pl.BlockSpec((1, TOK_TILE, W), lambda b, i: (b, i, 0))
    return pl.pallas_call(
        _rw_out_kernel,
        grid=(NB, S // TOK_TILE),
        in_specs=[tile] * 4 + [
            pl.BlockSpec((1, TOK_TILE, W), lambda b, i: (b, i, U_LORA)),
            pl.BlockSpec((W, W), lambda b, i: (0, 0)),
            pl.BlockSpec((W, W), lambda b, i: (0, 0)),
            pl.BlockSpec((8, W), lambda b, i: (0, 0)),
        ],
        out_specs=tile,
        out_shape=jax.ShapeDtypeStruct((NB, S, W), BF),
        compiler_params=_params(("arbitrary", "arbitrary"), 32),
        name="rw_out",
    )(y0, y1, bon0, bon1, u3, g2e, avg_bd, ln)


def _dft_mats():
    gd = W // 4
    cc = np.arange(gd)
    ang = 2 * np.pi * ((cc[:, None] * cc[None, :]) % gd) / gd
    chan = np.zeros((W, 2 * W), np.float32)
    for g in range(4):
        chan[g * gd:(g + 1) * gd, g * gd:(g + 1) * gd] = np.cos(ang)
        chan[g * gd:(g + 1) * gd, W + g * gd:W + (g + 1) * gd] = np.sin(ang)
    cos = np.zeros((S, S), np.float32)
    nsin = np.zeros((S, S), np.float32)
    scale = np.zeros((S, 1), np.float32)
    for lo, n in ((0, CTX), (CTX, T)):
        tt = np.arange(n)
        a = 2 * np.pi * ((tt[:, None] * tt[None, :]) % n) / n
        cos[lo:lo + n, lo:lo + n] = np.cos(a)
        nsin[lo:lo + n, lo:lo + n] = -np.sin(a)
        scale[lo:lo + n] = (n * gd) ** -0.5
    return chan, cos, nsin, scale


def _fn_tok_kernel(cos_ref, nsin_ref, z_ref, sc_ref, o_ref):
    z = z_ref[0]
    f = _dot(cos_ref[...], z[:, 0:W]) + _dot(nsin_ref[...], z[:, W:2 * W])
    o_ref[0] = (f * sc_ref[...]).astype(o_ref.dtype)


def _fourier(u2, chan, cos, nsin, scale):
    zcs = _matmul(u2, chan, tm=1024, tn=2 * W, out_dtype=BF, a_col_blk=U_FN_X, k=W, name="fn_chan")
    return pl.pallas_call(
        _fn_tok_kernel,
        grid=(NB, S // TOK_TILE),
        in_specs=[
            pl.BlockSpec((TOK_TILE, S), lambda b, i: (i, 0)),
            pl.BlockSpec((TOK_TILE, S), lambda b, i: (i, 0)),
            pl.BlockSpec((1, S, 2 * W), lambda b, i: (b, 0, 0)),
            pl.BlockSpec((TOK_TILE, 1), lambda b, i: (i, 0)),
        ],
        out_specs=pl.BlockSpec((1, TOK_TILE, W), lambda b, i: (b, i, 0)),
        out_shape=jax.ShapeDtypeStruct((NB, S, W), BF),
        compiler_params=_params(("arbitrary", "arbitrary"), 40),
        name="fn_tok",
    )(cos, nsin, zcs.reshape(NB, S, 2 * W), scale)


def _sc_kernel(b_ref, c_ref, x_ref, cp_ref, xp_ref, cn_ref, xn_ref, w_ref, o_ref):
    i = pl.program_id(1)
    n_t = S // TOK_TILE
    z = c_ref[0] * x_ref[0]
    zp = _shift_prev(z, cp_ref[0, 7:8, :] * xp_ref[0, 7:8, :], i <= 1, False)
    zn = _shift_prev(z, cn_ref[0, 0:1, :] * xn_ref[0, 0:1, :], (i == 0) | (i == n_t - 1), True)
    w = w_ref[...]
    o_ref[0] = (b_ref[0] * (zp * w[0:1] + z * w[1:2] + zn * w[2:3])).astype(o_ref.dtype)


def _short_conv(u3, conv_w):
    hb = TOK_TILE // 8

    def tile(blk):
        return pl.BlockSpec((1, TOK_TILE, W), lambda b, i: (b, i, blk))

    def prev(blk):
        return pl.BlockSpec((1, 8, W), lambda b, i: (b, jnp.maximum(i * hb - 1, 0), blk))

    def nxt(blk):
        return pl.BlockSpec((1, 8, W), lambda b, i: (b, jnp.minimum((i + 1) * hb, S // 8 - 1), blk))

    return pl.pallas_call(
        _sc_kernel,
        grid=(NB, S // TOK_TILE),
        in_specs=[tile(U_SC_B), tile(U_SC_C), tile(U_SC_X), prev(U_SC_C), prev(U_SC_X), nxt(U_SC_C), nxt(U_SC_X),
                  pl.BlockSpec((8, W), lambda b, i: (0, 0))],
        out_specs=pl.BlockSpec((1, TOK_TILE, W), lambda b, i: (b, i, 0)),
        out_shape=jax.ShapeDtypeStruct((NB, S, W), BF),
        compiler_params=_params(("arbitrary", "arbitrary"), 32),
        name="short_conv",
    )(*([u3] * 7), conv_w)


def _merge_kernel(y0_ref, y1_ref, y2_ref, y3_ref, g0_ref, g1_ref, g2_ref, g3_ref, w_ref, o_ref):
    acc = None
    for n, (y_ref, g_ref) in enumerate(((y0_ref, g0_ref), (y1_ref, g1_ref), (y2_ref, g2_ref), (y3_ref, g3_ref))):
        t = jax.nn.sigmoid(g_ref[...]) * _dot(y_ref[...], w_ref[n])
        acc = t if acc is None else acc + t
    o_ref[...] = acc.astype(o_ref.dtype)


def _merge(branches, u2, w_branch):
    tm, tn = 512, 512

    def gate(n):
        return pl.BlockSpec((tm, tn), lambda i, j: (i, U_GATES + n * (D // tn) + j))

    return pl.pallas_call(
        _merge_kernel,
        grid=(M // tm, D // tn),
        in_specs=[pl.BlockSpec((tm, W), lambda i, j: (i, 0))] * 4 + [gate(n) for n in range(4)] + [
            pl.BlockSpec((4, W, tn), lambda i, j: (0, 0, j))],
        out_specs=pl.BlockSpec((tm, tn), lambda i, j: (i, j)),
        out_shape=jax.ShapeDtypeStruct((M, D), BF),
        compiler_params=_params(("arbitrary", "arbitrary"), 32),
        name="merge",
    )(*branches, u2, u2, u2, u2, w_branch)


def _ffn_kernel(te_ref, nv_ref, x_ref, w1_ref, w3_ref, w2_ref, s_ref, o_ref):
    i, f = pl.program_id(0), pl.program_id(1)

    @pl.when(f == 0)
    def _():
        o_ref[...] = jnp.zeros_like(o_ref)

    @pl.when(i < nv_ref[0])
    def _():
        x = x_ref[...]
        a = _dot(x, w1_ref[0].astype(BF))
        b = _dot(x, w3_ref[0].astype(BF))
        g = (a * jax.nn.sigmoid(a) * b).astype(BF)
        o_ref[...] += _dot(g, w2_ref[0].astype(BF))

        @pl.when(f == pl.num_programs(1) - 1)
        def _():
            o_ref[...] *= s_ref[...]


def _ffn(x, w1, w3, w2, tile_expert, n_valid, row_scale, *, tm, tf):
    rows = x.shape[0]
    n_tiles = rows // tm

    def tile(i, nv):
        return jnp.minimum(i, nv[0] - 1)

    return pl.pallas_call(
        _ffn_kernel,
        grid_spec=pltpu.PrefetchScalarGridSpec(
            num_scalar_prefetch=2,
            grid=(n_tiles, D_FF // tf),
            in_specs=[
                pl.BlockSpec((tm, D), lambda i, f, te, nv: (tile(i, nv), 0)),
                pl.BlockSpec((1, D, tf), lambda i, f, te, nv: (te[tile(i, nv)], 0, f)),
                pl.BlockSpec((1, D, tf), lambda i, f, te, nv: (te[tile(i, nv)], 0, f)),
                pl.BlockSpec((1, tf, D), lambda i, f, te, nv: (te[tile(i, nv)], f, 0)),
                pl.BlockSpec((tm, 1), lambda i, f, te, nv: (tile(i, nv), 0)),
            ],
            out_specs=pl.BlockSpec((tm, D), lambda i, f, te, nv: (i, 0)),
        ),
        out_shape=jax.ShapeDtypeStruct((rows, D), F32),
        compiler_params=_params(("arbitrary", "arbitrary"), 56),
        name="ffn",
    )(tile_expert, n_valid, x, w1, w3, w2, row_scale)


def _moe(hf, logits, w1, w3, w2):
    lg = logits[:, :N_EXP]
    top_val, top_idx = lax.top_k(lg, 2)
    top_p = jax.nn.softmax(top_val, axis=-1)
    e_flat = top_idx.reshape(-1)
    onehot = (e_flat[:, None] == jnp.arange(N_EXP)[None, :]).astype(jnp.int32)
    rank = jnp.take_along_axis(jnp.cumsum(onehot, axis=0) - onehot, e_flat[:, None], axis=1)[:, 0]
    count = jnp.sum(onehot, axis=0)
    tiles_per = (count + MOE_TM - 1) // MOE_TM
    tile_end = jnp.cumsum(tiles_per)
    group_start = (tile_end - tiles_per) * MOE_TM
    pos = group_start[e_flat] + rank
    n_valid = tile_end[-1:].astype(jnp.int32)
    tile_expert = jnp.minimum(jnp.searchsorted(tile_end, jnp.arange(MOE_TILES), side="right"),
                              N_EXP - 1).astype(jnp.int32)
    rows = MOE_TILES * MOE_TM
    src = jnp.zeros((rows,), jnp.int32).at[pos].set(jnp.arange(2 * M, dtype=jnp.int32) // 2)
    scale = jnp.zeros((rows, 1), F32).at[pos, 0].set(top_p.reshape(-1))
    xs = jnp.take(hf, src, axis=0)
    ys = _ffn(xs, w1, w3, w2, tile_expert, n_valid, scale, tm=MOE_TM, tf=256)
    pos2 = pos.reshape(M, 2)
    return jnp.take(ys, pos2[:, 0], axis=0) + jnp.take(ys, pos2[:, 1], axis=0)


def _regroup_w_in(w):
    pad = jnp.zeros((D, 128), w.dtype)
    return jnp.concatenate([w[:, 0:2048], w[:, 2048:2304], w[:, 3328:3456], pad, w[:, 2304:3328], w[:, 3456:]],
                           axis=1).astype(BF)


def _rows_at(mat, off):
    return jnp.zeros((W, W), F32).at[off:off + mat.shape[0]].set(mat)


def _pad_rows(rows, n):
    z = jnp.stack(rows)
    return jnp.concatenate([z, jnp.zeros((n - z.shape[0],) + z.shape[1:], z.dtype)], axis=0)


def kernel(x, c, ctx, c_ctx, ada_w, ada_b, norm_mix, norm_ffn, w_in, na_rel_bias, rw_mu_kvr, rw_mu_lora, rw_w0, rw_w2, rw_a0, rw_a2, rw_k_k, rw_k_a, rw_r_k, rw_g2, rw_ln_w, rw_ln_b, sc_conv, w_branch, w_out, ffn_w1, ffn_w3, ffn_w2, moe_router, moe_w1, moe_w3, moe_w2, norm_final):
    xs = jnp.concatenate([ctx, x], axis=1)
    cvec = jnp.concatenate([c, c_ctx[None], jnp.zeros((3, D), F32)], axis=0)
    mods = _ada(cvec, ada_w, ada_b).reshape(DEPTH, 8, 6, D)

    def mod_table(gate, scale, shift):
        def pick(sel):
            if sel is None:
                return jnp.zeros((NB, 2, D), F32)
            v = mods[sel[0], :, sel[1]]
            return jnp.stack([jnp.broadcast_to(v[4], (NB, D)), v[:NB]], axis=1)
        t = jnp.stack([pick(gate), pick(scale), pick(shift)], axis=2)
        return jnp.concatenate([t, jnp.zeros((NB, 2, 5, D), F32)], axis=2)

    head_id = np.arange(W) // HD
    ones_bd = jnp.asarray((head_id[:, None] == head_id[None, :]).astype(np.float32))
    chan, cos, nsin, fscale = _dft_mats()
    chan, cos, nsin, fscale = jnp.asarray(chan, BF), jnp.asarray(cos, BF), jnp.asarray(nsin, BF), jnp.asarray(fscale)

    delta = None
    for l in range(DEPTH):
        mod = mod_table(None if l == 0 else (l - 1, 5), (l, 1), (l, 0))
        if l == 0:
            (h,) = _resid_norm(xs, None, mod, norm_mix[l])
        else:
            xs, h = _resid_norm(xs, delta, mod, norm_mix[l])
        u2 = _matmul(h.reshape(M, D), _regroup_w_in(w_in[l]), tm=512, tn=1536, out_dtype=F32, name="w_in")
        u3 = u2.reshape(NB, S, U_COLS)

        y_na = _attention(u3, _na_bias_table(na_rel_bias[l]))

        lora_mu = jnp.zeros((2, W), F32)
        lora_mu = lora_mu.at[:, LORA_W:LORA_W + 2 * LORA].set(
            jnp.broadcast_to(rw_mu_lora[l][:, 0].reshape(1, 2 * LORA), (2, 2 * LORA)))
        lora_mu = lora_mu.at[:, LORA_A:LORA_A + 2 * LORA].set(
            jnp.broadcast_to(rw_mu_lora[l][:, 1].reshape(1, 2 * LORA), (2, 2 * LORA)))
        vec = jnp.stack([_pad_rows([rw_mu_kvr[l][d, 0], rw_mu_kvr[l][d, 1], rw_mu_kvr[l][d, 2], lora_mu[d],
                                    rw_w0[l][d], rw_a0[l][d], rw_k_k[l], rw_k_a[l], rw_r_k[l]], 16)
                         for d in range(2)])
        w2e = jnp.stack([_rows_at(rw_w2[l][d], LORA_W + LORA * d) for d in range(2)])
        a2e = jnp.stack([_rows_at(rw_a2[l][d], LORA_A + LORA * d) for d in range(2)])
        ys, bons = [], []
        for d in range(2):
            lw, k2, v1, kk, kb, r1, bon = _rw_prep(u3, vec, w2e, a2e, ones_bd, d)
            ys.append(_rw_scan((lw, k2, v1, kk, kb, r1), d))
            bons.append(bon)
        y_rw = _rw_out(ys[0], ys[1], bons[0], bons[1], u3, _rows_at(rw_g2[l], LORA_G).astype(BF),
                       ones_bd / HD, _pad_rows([rw_ln_w[l], rw_ln_b[l]], 8))

        y_fn = _fourier(u2, chan, cos, nsin, fscale)
        y_sc = _short_conv(u3, _pad_rows([sc_conv[l][0], sc_conv[l][1], sc_conv[l][2]], 8))

        merged = _merge([z.reshape(M, W) for z in (y_na, y_rw, y_fn, y_sc)], u2, w_branch[l].astype(BF))
        y = _matmul(merged, w_out[l].astype(BF), tm=512, tn=2048, out_dtype=F32, name="w_out")

        mod = mod_table((l, 2), (l, 4), (l, 3))
        i = l // 2
        if l % 2 == 0:
            xs, hf = _resid_norm(xs, y.reshape(NB, S, D), mod, norm_ffn[l])
            delta = _ffn(hf.reshape(M, D), ffn_w1[i][None], ffn_w3[i][None], ffn_w2[i][None],
                         jnp.zeros((M // 1024,), jnp.int32), jnp.full((1,), M // 1024, jnp.int32),
                         jnp.ones((M, 1), F32), tm=1024, tf=256)
        else:
            router = jnp.concatenate([moe_router[i], jnp.zeros((D, 128 - N_EXP), F32)], axis=1)
            xs, hf, logits = _resid_norm(xs, y.reshape(NB, S, D), mod, norm_ffn[l], router=router)
            delta = _moe(hf.reshape(M, D), logits.reshape(M, 128), moe_w1[i], moe_w3[i], moe_w2[i])
        delta = delta.reshape(NB, S, D)

    mod = mod_table((DEPTH - 1, 5), None, None)
    (out,) = _resid_norm(xs, delta, mod, norm_final, final=True)
    return out
```

```python
import functools

import jax
import jax.numpy as jnp
import numpy as np
from jax import lax
from jax.experimental import pallas as pl
from jax.experimental.pallas import tpu as pltpu

BF = jnp.bfloat16
F32 = jnp.float32
HI = lax.Precision.HIGHEST

D = 2048
NB = 4
T = 2048
CTX = 256
S = CTX + T
M = NB * S
DEPTH = 4
GRID_W = 64
ROWS = T // GRID_W
WIN_H = 8
WIN_W = 16
HEADS = 8
HD = 64
W = HEADS * HD
LORA = 64
D_FF = 5632
N_EXP = 8
RMS_EPS = 1e-6
GN_EPS = 64e-5
NEG = -1e30

U_NA_K, U_NA_V, U_RW_K, U_RW_V, U_LORA, U_NA_Q, U_RW_R, U_FN_X, U_SC_B, U_SC_C, U_SC_X, U_GATES = range(12)
U_COLS = 27 * 512
LORA_W, LORA_A, LORA_G = 0, 128, 256

CHUNK = 64
N_CHUNK = S // CHUNK
CTX_CHUNKS = CTX // CHUNK
TOK_TILE = 256
MOE_TM = 1024
MOE_TILES = (2 * M) // MOE_TM + N_EXP
GATHER_ROWS = 256


def _params(sem, vmem_mb):
    return pltpu.CompilerParams(dimension_semantics=sem, vmem_limit_bytes=vmem_mb << 20)


def _dot(a, b, **kw):
    return jnp.dot(a, b, preferred_element_type=F32, **kw)


def _dot_nt(a, b):
    return lax.dot_general(a, b, (((1,), (1,)), ((), ())), preferred_element_type=F32)


def _dot_tn(a, b):
    return lax.dot_general(a, b, (((0,), (0,)), ((), ())), preferred_element_type=F32)


def _ada_kernel(c_ref, w_ref, b_ref, o_ref):
    c = c_ref[...]
    a = (c * jax.nn.sigmoid(c)).astype(BF)
    o_ref[0] = _dot(a, w_ref[0].astype(BF)) + b_ref[0]


def _ada(cvec, ada_w, ada_b):
    tn = 1024
    return pl.pallas_call(
        _ada_kernel,
        grid=(DEPTH, 6 * D // tn),
        in_specs=[
            pl.BlockSpec((8, D), lambda l, j: (0, 0)),
            pl.BlockSpec((1, D, tn), lambda l, j: (l, 0, j)),
            pl.BlockSpec((1, 1, tn), lambda l, j: (l, 0, j)),
        ],
        out_specs=pl.BlockSpec((1, 8, tn), lambda l, j: (l, 0, j)),
        out_shape=jax.ShapeDtypeStruct((DEPTH, 8, 6 * D), F32),
        compiler_params=_params(("arbitrary", "arbitrary"), 40),
        name="ada",
    )(cvec, ada_w, ada_b.reshape(DEPTH, 1, 6 * D))


def _rn_kernel(*refs, has_delta, has_router, final):
    it = iter(refs)
    x_ref = next(it)
    d_ref = next(it) if has_delta else None
    mod_ref = next(it)
    g_ref = next(it)
    r_ref = next(it) if has_router else None
    xo_ref = next(it) if (has_delta and not final) else None
    h_ref = next(it)
    lg_ref = next(it) if has_router else None
    x = x_ref[0]
    mod = mod_ref[0, 0]
    if has_delta:
        x = x + mod[0:1] * d_ref[0]
        if xo_ref is not None:
            xo_ref[0] = x
    y = x * lax.rsqrt(jnp.mean(x * x, axis=-1, keepdims=True) + RMS_EPS) * g_ref[...]
    if final:
        h_ref[0] = y
        return
    h = y * (1.0 + mod[1:2]) + mod[2:3]
    h_ref[0] = h.astype(h_ref.dtype)
    if has_router:
        lg_ref[0] = _dot(h, r_ref[...], precision=HI)


def _resid_norm(x, delta, mod, g, router=None, final=False):
    has_delta = delta is not None
    has_router = router is not None
    n_t = S // TOK_TILE
    off = CTX // TOK_TILE if final else 0
    grid = (NB, n_t - off)
    tok = pl.BlockSpec((1, TOK_TILE, D), lambda b, i: (b, i + off, 0))
    in_specs = [tok] + ([tok] if has_delta else [])
    in_specs += [
        pl.BlockSpec((1, 1, 8, D), lambda b, i: (b, jnp.minimum(i + off, 1), 0, 0)),
        pl.BlockSpec((1, D), lambda b, i: (0, 0)),
    ]
    args = [x] + ([delta] if has_delta else []) + [mod, g.reshape(1, D)]
    out_specs, out_shape = [], []
    if final:
        out_specs.append(pl.BlockSpec((1, TOK_TILE, D), lambda b, i: (b, i, 0)))
        out_shape.append(jax.ShapeDtypeStruct((NB, T, D), F32))
    else:
        if has_delta:
            out_specs.append(tok)
            out_shape.append(jax.ShapeDtypeStruct((NB, S, D), F32))
        out_specs.append(tok)
        out_shape.append(jax.ShapeDtypeStruct((NB, S, D), F32 if has_router else BF))
    if has_router:
        in_specs.append(pl.BlockSpec((D, 128), lambda b, i: (0, 0)))
        args.append(router)
        out_specs.append(pl.BlockSpec((1, TOK_TILE, 128), lambda b, i: (b, i, 0)))
        out_shape.append(jax.ShapeDtypeStruct((NB, S, 128), F32))
    return pl.pallas_call(
        functools.partial(_rn_kernel, has_delta=has_delta, has_router=has_router, final=final),
        grid=grid,
        in_specs=in_specs,
        out_specs=out_specs,
        out_shape=out_shape,
        compiler_params=_params(("arbitrary", "arbitrary"), 40),
        name="resid_norm",
    )(*args)


def _mm_kernel(a_ref, w_ref, o_ref):
    o_ref[...] = _dot(a_ref[...].astype(BF), w_ref[...].astype(BF)).astype(o_ref.dtype)


def _matmul(a, w, *, tm, tn, out_dtype, a_col_blk=0, k=None, name="matmul"):
    m = a.shape[0]
    k = k or a.shape[1]
    n = w.shape[1]
    return pl.pallas_call(
        _mm_kernel,
        grid=(n // tn, m // tm),
        in_specs=[
            pl.BlockSpec((tm, k), lambda j, i: (i, a_col_blk)),
            pl.BlockSpec((k, tn), lambda j, i: (0, j)),
        ],
        out_specs=pl.BlockSpec((tm, tn), lambda j, i: (i, j)),
        out_shape=jax.ShapeDtypeStruct((m, n), out_dtype),
        compiler_params=_params(("arbitrary", "arbitrary"), 48),
        name=name,
    )(a, w)


def _na_window(r):
    rr = r - CTX // GRID_W
    rs = jnp.clip(rr - WIN_H // 2, 0, ROWS - WIN_H)
    return rr, rs


def _na_kernel(q_ref, k_ref, v_ref, bias_ref, o_ref):
    _, rs = _na_window(pl.program_id(1))
    kstart = pl.multiple_of(CTX + GRID_W * rs, GRID_W)
    lo = lax.broadcasted_iota(jnp.int32, (GRID_W, 128), 1) < HD
    n_nb = WIN_H * GRID_W
    for p in range(HEADS // 2):
        cs = slice(128 * p, 128 * p + 128)
        q = q_ref[0, :, cs] * (HD ** -0.5)
        q2 = jnp.concatenate([jnp.where(lo, q, 0.0), jnp.where(lo, 0.0, q)], axis=0).astype(BF)
        kn = k_ref[0, pl.ds(kstart, n_nb), cs].astype(BF)
        kc = k_ref[0, 0:CTX, cs].astype(BF)
        s_nb = _dot_nt(q2, kn) + bias_ref[0, 128 * p:128 * p + 128, :]
        s_cx = _dot_nt(q2, kc)
        m = jnp.maximum(jnp.max(s_nb, axis=-1, keepdims=True), jnp.max(s_cx, axis=-1, keepdims=True))
        e_nb = jnp.exp(s_nb - m)
        e_cx = jnp.exp(s_cx - m)
        l = jnp.sum(e_nb, axis=-1, keepdims=True) + jnp.sum(e_cx, axis=-1, keepdims=True)
        vn = v_ref[0, pl.ds(kstart, n_nb), cs].astype(BF)
        vc = v_ref[0, 0:CTX, cs].astype(BF)
        o2 = (_dot(e_nb.astype(BF), vn) + _dot(e_cx.astype(BF), vc)) / l
        o_ref[0, :, cs] = jnp.where(lo, o2[0:GRID_W], o2[GRID_W:]).astype(o_ref.dtype)


def _na_bias_table(rel_bias):
    cols = np.arange(GRID_W)
    col_start = np.clip(cols - WIN_W // 2, 0, GRID_W - WIN_W)
    col_ok = (cols[None, :] >= col_start[:, None]) & (cols[None, :] < col_start[:, None] + WIN_W)
    dc = np.clip(cols[None, :] - cols[:, None] + (WIN_W - 1), 0, 2 * WIN_W - 2)
    d0 = np.arange(WIN_H)[:, None] + np.arange(WIN_H)[None, :]
    tab = rel_bias[:, d0[:, :, None, None], dc[None, None, :, :]]
    tab = jnp.where(col_ok[None, None, None], tab.astype(F32), NEG)
    tab = jnp.transpose(tab, (1, 0, 3, 2, 4)).reshape(WIN_H, HEADS * GRID_W, WIN_H * GRID_W)
    return jnp.concatenate([tab, jnp.full((1,) + tab.shape[1:], NEG, F32)], axis=0)


def _attention(u3, bias_tab):
    def bias_idx(b, r):
        rr, rs = _na_window(r)
        return (jnp.where(rr < 0, WIN_H, rs - rr + WIN_H - 1), 0, 0)

    return pl.pallas_call(
        _na_kernel,
        grid=(NB, S // GRID_W),
        in_specs=[
            pl.BlockSpec((1, GRID_W, W), lambda b, r: (b, r, U_NA_Q)),
            pl.BlockSpec((1, S, W), lambda b, r: (b, 0, U_NA_K)),
            pl.BlockSpec((1, S, W), lambda b, r: (b, 0, U_NA_V)),
            pl.BlockSpec((1, HEADS * GRID_W, WIN_H * GRID_W), bias_idx),
        ],
        out_specs=pl.BlockSpec((1, GRID_W, W), lambda b, r: (b, r, 0)),
        out_shape=jax.ShapeDtypeStruct((NB, S, W), BF),
        compiler_params=_params(("arbitrary", "arbitrary"), 48),
        name="attention",
    )(u3, u3, u3, bias_tab)


def _shift_prev(z, halo_row, first_is_start, rev):
    n = z.shape[0]
    row = lax.broadcasted_iota(jnp.int32, z.shape, 0)
    edge = jnp.where(first_is_start, 0.0, halo_row)
    if rev:
        return jnp.where(row == n - 1, edge, pltpu.roll(z, n - 1, 0))
    return jnp.where(row == 0, edge, pltpu.roll(z, 1, 0))


def _rw_prep_kernel(k_ref, v_ref, r_ref, l_ref, kh_ref, vh_ref, rh_ref, lh_ref, vec_ref, w2_ref, a2_ref,
                    ones_ref, lw_o, k_o, v_o, kk_o, b_o, r_o, bonus_o, *, rev):
    i = pl.program_id(1)
    n_t = S // TOK_TILE
    start = (i == n_t - 1) | (i == 0) if rev else (i <= 1)
    hrow = 0 if rev else 7
    vec = vec_ref[0]
    mu_k, mu_v, mu_r, mu_l, w0, a0, k_k, k_a, r_k = (vec[j:j + 1] for j in range(9))

    def lerp(ref, href, mu):
        z = ref[0]
        prev = _shift_prev(z, href[0, hrow:hrow + 1, :], start, rev)
        return z + (prev - z) * mu

    lz = lerp(l_ref, lh_ref, mu_l)
    wz = w0 + _dot(jnp.tanh(lz), w2_ref[0], precision=HI)
    a = jax.nn.sigmoid(a0 + _dot(lz, a2_ref[0], precision=HI))
    softplus = jnp.maximum(-wz, 0.0) + jnp.log1p(jnp.exp(-jnp.abs(wz)))
    lw_o[0] = -jnp.exp(-softplus - 0.5)
    k1 = lerp(k_ref, kh_ref, mu_k)
    kkr = k1 * k_k
    ones = ones_ref[...]
    nrm = jnp.sqrt(_dot(kkr * kkr, ones, precision=HI))
    kk = kkr / jnp.maximum(nrm, 1e-12)
    k2 = k1 * (1.0 + (a - 1.0) * k_a)
    v1 = lerp(v_ref, vh_ref, mu_v)
    r1 = lerp(r_ref, rh_ref, mu_r)
    k_o[0] = k2
    v_o[0] = v1
    kk_o[0] = kk
    b_o[0] = kk * a
    r_o[0] = r1
    bonus_o[0] = _dot(r1 * k2 * r_k, ones, precision=HI) * v1


def _rw_prep(u3, vec, w2e, a2e, ones_bd, d):
    rev = d == 1
    n_t = S // TOK_TILE
    hb = TOK_TILE // 8

    def tile(blk):
        return pl.BlockSpec((1, TOK_TILE, W), lambda b, i: (b, i, blk))

    def halo(blk):
        if rev:
            return pl.BlockSpec((1, 8, W), lambda b, i: (b, jnp.minimum((i + 1) * hb, S // 8 - 1), blk))
        return pl.BlockSpec((1, 8, W), lambda b, i: (b, jnp.maximum(i * hb - 1, 0), blk))

    blks = (U_RW_K, U_RW_V, U_RW_R, U_LORA)
    out = pl.BlockSpec((1, TOK_TILE, W), lambda b, i: (b, i, 0))
    return pl.pallas_call(
        functools.partial(_rw_prep_kernel, rev=rev),
        grid=(NB, n_t),
        in_specs=[tile(x) for x in blks] + [halo(x) for x in blks] + [
            pl.BlockSpec((1, 16, W), lambda b, i: (d, 0, 0)),
            pl.BlockSpec((1, W, W), lambda b, i: (d, 0, 0)),
            pl.BlockSpec((1, W, W), lambda b, i: (d, 0, 0)),
            pl.BlockSpec((W, W), lambda b, i: (0, 0)),
        ],
        out_specs=[out] * 7,
        out_shape=[jax.ShapeDtypeStruct((NB, S, W), F32)] * 7,
        compiler_params=_params(("arbitrary", "arbitrary"), 48),
        name="rw_prep",
    )(*([u3] * 8), vec, w2e, a2e, ones_bd)


def _pair_rows(z, lo):
    return jnp.concatenate([jnp.where(lo, z, 0.0), jnp.where(lo, 0.0, z)], axis=0)


def _rw_scan_kernel(*refs):
    ins, y_refs, s_ref = (refs[0:6], refs[6:12]), refs[12:14], refs[14]

    @pl.when(pl.program_id(1) == 0)
    def _():
        s_ref[...] = jnp.zeros_like(s_ref)

    n = CHUNK
    t_i = lax.broadcasted_iota(jnp.int32, (n, n), 0)
    s_i = lax.broadcasted_iota(jnp.int32, (n, n), 1)
    t2 = lax.broadcasted_iota(jnp.int32, (2 * n, 2 * n), 0)
    s2 = lax.broadcasted_iota(jnp.int32, (2 * n, 2 * n), 1)
    same = (t2 >= n) == (s2 >= n)
    tt, ss = t2 & (n - 1), s2 & (n - 1)
    eye2 = t2 == s2
    lo = lax.broadcasted_iota(jnp.int32, (n, 128), 1) < HD
    incl1 = (s_i <= t_i, s_i >= t_i)
    incl2 = (same & (ss <= tt), same & (ss >= tt))
    strict2 = (same & (ss < tt), same & (ss > tt))
    last = (n - 1, 0)
    bf = lambda z: z.astype(BF)

    lw_all = [ins[d][0][0] for d in range(2)]
    cum_all = [_dot(incl1[d].astype(F32), lw_all[d], precision=HI) for d in range(2)]
    chains = [(d, p) for d in range(2) for p in range(HEADS // 2)]

    def operands(d, p):
        cs = slice(128 * p, 128 * p + 128)
        _, k_ref, v_ref, kk_ref, b_ref, r_ref = ins[d]
        lw, cum = lw_all[d][:, cs], cum_all[d][:, cs]
        tot = cum[last[d]:last[d] + 1, :]
        g, gp, gi, gc = jnp.exp(cum), jnp.exp(cum - lw), jnp.exp(-cum), jnp.exp(tot - cum)
        kk, b, k, r, v = kk_ref[0, :, cs], b_ref[0, :, cs], k_ref[0, :, cs], r_ref[0, :, cs], v_ref[0, :, cs]
        return dict(
            a2=bf(_pair_rows(-kk * gp, lo)), r2=_pair_rows(r * g, lo), b2=bf(_pair_rows(b * gi, lo)),
            k2=bf(_pair_rows(k * gi, lo)), bt2=bf(_pair_rows(b * gc, lo)), kt2=bf(_pair_rows(k * gc, lo)),
            v2=bf(_pair_rows(v, lo)), gtot=jnp.exp(tot))

    op = [operands(d, p) for d, p in chains]
    sc = [_dot_nt(jnp.concatenate([o["a2"], bf(o["r2"])], axis=0), jnp.concatenate([o["b2"], o["k2"]], axis=0))
          for o in op]
    m_ab = [jnp.where(strict2[d], s[:2 * n, :2 * n], 0.0) for (d, _), s in zip(chains, sc)]
    m_ak = [bf(jnp.where(strict2[d], s[:2 * n, 2 * n:], 0.0)) for (d, _), s in zip(chains, sc)]
    m_rb = [bf(jnp.where(incl2[d], s[2 * n:, :2 * n], 0.0)) for (d, _), s in zip(chains, sc)]
    m_rk = [bf(jnp.where(incl2[d], s[2 * n:, 2 * n:], 0.0)) for (d, _), s in zip(chains, sc)]
    x = [jnp.where(eye2, 1.0, 0.0) + m for m in m_ab]
    pw = m_ab
    for _ in range(int(np.log2(n)) - 1):
        pw = [_dot(bf(z), bf(z)) for z in pw]
        x = [xi + _dot(bf(xi), bf(z)) for xi, z in zip(x, pw)]
    xb = [bf(z) for z in x]
    mv = [bf(_dot(m, o["v2"])) for m, o in zip(m_ak, op)]
    wmb = [bf(_dot(z, o["a2"])) for z, o in zip(xb, op)]
    ub = [bf(_dot(z, m)) for z, m in zip(xb, mv)]
    rm = [o["r2"] + _dot(m, w) for o, m, w in zip(op, m_rb, wmb)]
    yu = [_dot(mb, u) + _dot(mk, o["v2"]) for mb, mk, u, o in zip(m_rb, m_rk, ub, op)]
    tm = [jnp.where(eye2, o["gtot"], 0.0) + _dot_tn(o["bt2"], w) for o, w in zip(op, wmb)]
    sv = [_dot_tn(o["bt2"], u) + _dot_tn(o["kt2"], o["v2"]) for o, u in zip(op, ub)]
    st = [bf(s_ref[d, p]) for d, p in chains]
    y2 = [_dot(bf(r), s) + y for r, s, y in zip(rm, st, yu)]
    s_new = [_dot(bf(t), s) + v for t, s, v in zip(tm, st, sv)]
    for (d, p), y, s in zip(chains, y2, s_new):
        y_refs[d][0, :, 128 * p:128 * p + 128] = y[:n] + y[n:]
        s_ref[d, p] = s


def _rw_scan(fwd, rev):
    def rev_idx(b, c):
        return (b, jnp.where(c < CTX_CHUNKS, CTX_CHUNKS - 1 - c, N_CHUNK + CTX_CHUNKS - 1 - c), 0)

    f_spec = pl.BlockSpec((1, CHUNK, W), lambda b, c: (b, c, 0))
    r_spec = pl.BlockSpec((1, CHUNK, W), rev_idx)
    return pl.pallas_call(
        _rw_scan_kernel,
        grid=(NB, N_CHUNK),
        in_specs=[f_spec] * 6 + [r_spec] * 6,
        out_specs=[f_spec, r_spec],
        out_shape=[jax.ShapeDtypeStruct((NB, S, W), F32)] * 2,
        scratch_shapes=[pltpu.VMEM((2, HEADS // 2, 128, 128), F32)],
        compiler_params=_params(("arbitrary", "arbitrary"), 32),
        name="rw_scan",
    )(*fwd, *rev)


def _rw_out_kernel(y0_ref, y1_ref, b0_ref, b1_ref, l_ref, g2_ref, avg_ref, ln_ref, o_ref):
    y = y0_ref[0] + y1_ref[0]
    avg = avg_ref[...]
    mu = _dot(y, avg, precision=HI)
    yc = y - mu
    var = _dot(yc * yc, avg, precision=HI)
    ln = ln_ref[...]
    y = yc * lax.rsqrt(var + GN_EPS) * ln[0:1] + ln[1:2] + b0_ref[0] + b1_ref[0]
    gate = _dot(jax.nn.sigmoid(l_ref[0]).astype(BF), g2_ref[...])
    o_ref[0] = (y * gate).astype(o_ref.dtype)


def _rw_out(y0, y1, bon0, bon1, u3, g2e, avg_bd, ln):
    tile = pl.BlockSpec((1, TOK_TILE, W), lambda b, i: (b, i, 0))
    return pl.pallas_call(
        _rw_out_kernel,
        grid=(NB, S // TOK_TILE),
        in_specs=[tile] * 4 + [
            pl.BlockSpec((1, TOK_TILE, W), lambda b, i: (b, i, U_LORA)),
            pl.BlockSpec((W, W), lambda b, i: (0, 0)),
            pl.BlockSpec((W, W), lambda b, i: (0, 0)),
            pl.BlockSpec((8, W), lambda b, i: (0, 0)),
        ],
        out_specs=tile,
        out_shape=jax.ShapeDtypeStruct((NB, S, W), BF),
        compiler_params=_params(("arbitrary", "arbitrary"), 32),
        name="rw_out",
    )(y0, y1, bon0, bon1, u3, g2e, avg_bd, ln)


def _dft_mats():
    gd = W // 4
    cc = np.arange(gd)
    ang = 2 * np.pi * ((cc[:, None] * cc[None, :]) % gd) / gd
    chan = np.zeros((W, 2 * W), np.float32)
    for g in range(4):
        chan[g * gd:(g + 1) * gd, g * gd:(g + 1) * gd] = np.cos(ang)
        chan[g * gd:(g + 1) * gd, W + g * gd:W + (g + 1) * gd] = np.sin(ang)
    cos = np.zeros((S, S), np.float32)
    nsin = np.zeros((S, S), np.float32)
    scale = np.zeros((S, 1), np.float32)
    for lo, n in ((0, CTX), (CTX, T)):
        tt = np.arange(n)
        a = 2 * np.pi * ((tt[:, None] * tt[None, :]) % n) / n
        cos[lo:lo + n, lo:lo + n] = np.cos(a)
        nsin[lo:lo + n, lo:lo + n] = -np.sin(a)
        scale[lo:lo + n] = (n * gd) ** -0.5
    return chan, cos, nsin, scale


def _fn_tok_kernel(cos_ref, nsin_ref, z_ref, sc_ref, o_ref):
    z = z_ref[0]
    f = _dot(cos_ref[...], z[:, 0:W]) + _dot(nsin_ref[...], z[:, W:2 * W])
    o_ref[0] = (f * sc_ref[...]).astype(o_ref.dtype)


def _fourier(u2, chan, cos, nsin, scale):
    zcs = _matmul(u2, chan, tm=1024, tn=2 * W, out_dtype=BF, a_col_blk=U_FN_X, k=W, name="fn_chan")
    return pl.pallas_call(
        _fn_tok_kernel,
        grid=(NB, S // TOK_TILE),
        in_specs=[
            pl.BlockSpec((TOK_TILE, S), lambda b, i: (i, 0)),
            pl.BlockSpec((TOK_TILE, S), lambda b, i: (i, 0)),
            pl.BlockSpec((1, S, 2 * W), lambda b, i: (b, 0, 0)),
            pl.BlockSpec((TOK_TILE, 1), lambda b, i: (i, 0)),
        ],
        out_specs=pl.BlockSpec((1, TOK_TILE, W), lambda b, i: (b, i, 0)),
        out_shape=jax.ShapeDtypeStruct((NB, S, W), BF),
        compiler_params=_params(("arbitrary", "arbitrary"), 40),
        name="fn_tok",
    )(cos, nsin, zcs.reshape(NB, S, 2 * W), scale)


def _sc_kernel(b_ref, c_ref, x_ref, cp_ref, xp_ref, cn_ref, xn_ref, w_ref, o_ref):
    i = pl.program_id(1)
    n_t = S // TOK_TILE
    z = c_ref[0] * x_ref[0]
    zp = _shift_prev(z, cp_ref[0, 7:8, :] * xp_ref[0, 7:8, :], i <= 1, False)
    zn = _shift_prev(z, cn_ref[0, 0:1, :] * xn_ref[0, 0:1, :], (i == 0) | (i == n_t - 1), True)
    w = w_ref[...]
    o_ref[0] = (b_ref[0] * (zp * w[0:1] + z * w[1:2] + zn * w[2:3])).astype(o_ref.dtype)


def _short_conv(u3, conv_w):
    hb = TOK_TILE // 8

    def tile(blk):
        return pl.BlockSpec((1, TOK_TILE, W), lambda b, i: (b, i, blk))

    def prev(blk):
        return pl.BlockSpec((1, 8, W), lambda b, i: (b, jnp.maximum(i * hb - 1, 0), blk))

    def nxt(blk):
        return pl.BlockSpec((1, 8, W), lambda b, i: (b, jnp.minimum((i + 1) * hb, S // 8 - 1), blk))

    return pl.pallas_call(
        _sc_kernel,
        grid=(NB, S // TOK_TILE),
        in_specs=[tile(U_SC_B), tile(U_SC_C), tile(U_SC_X), prev(U_SC_C), prev(U_SC_X), nxt(U_SC_C), nxt(U_SC_X),
                  pl.BlockSpec((8, W), lambda b, i: (0, 0))],
        out_specs=pl.BlockSpec((1, TOK_TILE, W), lambda b, i: (b, i, 0)),
        out_shape=jax.ShapeDtypeStruct((NB, S, W), BF),
        compiler_params=_params(("arbitrary", "arbitrary"), 32),
        name="short_conv",
    )(*([u3] * 7), conv_w)


def _merge_kernel(y0_ref, y1_ref, y2_ref, y3_ref, g0_ref, g1_ref, g2_ref, g3_ref, w_ref, o_ref):
    acc = None
    for n, (y_ref, g_ref) in enumerate(((y0_ref, g0_ref), (y1_ref, g1_ref), (y2_ref, g2_ref), (y3_ref, g3_ref))):
        t = jax.nn.sigmoid(g_ref[...]) * _dot(y_ref[...], w_ref[n])
        acc = t if acc is None else acc + t
    o_ref[...] = acc.astype(o_ref.dtype)


def _merge(branches, u2, w_branch):
    tm, tn = 512, 512

    def gate(n):
        return pl.BlockSpec((tm, tn), lambda i, j: (i, U_GATES + n * (D // tn) + j))

    return pl.pallas_call(
        _merge_kernel,
        grid=(M // tm, D // tn),
        in_specs=[pl.BlockSpec((tm, W), lambda i, j: (i, 0))] * 4 + [gate(n) for n in range(4)] + [
            pl.BlockSpec((4, W, tn), lambda i, j: (0, 0, j))],
        out_specs=pl.BlockSpec((tm, tn), lambda i, j: (i, j)),
        out_shape=jax.ShapeDtypeStruct((M, D), BF),
        compiler_params=_params(("arbitrary", "arbitrary"), 32),
        name="merge",
    )(*branches, u2, u2, u2, u2, w_branch)


def _ffn_kernel(te_ref, nv_ref, x_ref, w1_ref, w3_ref, w2_ref, s_ref, o_ref):
    i, f = pl.program_id(0), pl.program_id(1)

    @pl.when(f == 0)
    def _():
        o_ref[...] = jnp.zeros_like(o_ref)

    @pl.when(i < nv_ref[0])
    def _():
        x = x_ref[...]
        a = _dot(x, w1_ref[0].astype(BF))
        b = _dot(x, w3_ref[0].astype(BF))
        g = (a * jax.nn.sigmoid(a) * b).astype(BF)
        o_ref[...] += _dot(g, w2_ref[0].astype(BF))

        @pl.when(f == pl.num_programs(1) - 1)
        def _():
            o_ref[...] *= s_ref[...]


def _ffn(x, w1, w3, w2, tile_expert, n_valid, row_scale, *, tm, tf):
    rows = x.shape[0]
    n_tiles = rows // tm

    def tile(i, nv):
        return jnp.minimum(i, nv[0] - 1)

    return pl.pallas_call(
        _ffn_kernel,
        grid_spec=pltpu.PrefetchScalarGridSpec(
            num_scalar_prefetch=2,
            grid=(n_tiles, D_FF // tf),
            in_specs=[
                pl.BlockSpec((tm, D), lambda i, f, te, nv: (tile(i, nv), 0)),
                pl.BlockSpec((1, D, tf), lambda i, f, te, nv: (te[tile(i, nv)], 0, f)),
                pl.BlockSpec((1, D, tf), lambda i, f, te, nv: (te[tile(i, nv)], 0, f)),
                pl.BlockSpec((1, tf, D), lambda i, f, te, nv: (te[tile(i, nv)], f, 0)),
                pl.BlockSpec((tm, 1), lambda i, f, te, nv: (tile(i, nv), 0)),
            ],
            out_specs=pl.BlockSpec((tm, D), lambda i, f, te, nv: (i, 0)),
        ),
        out_shape=jax.ShapeDtypeStruct((rows, D), F32),
        compiler_params=_params(("arbitrary", "arbitrary"), 56),
        name="ffn",
    )(tile_expert, n_valid, x, w1, w3, w2, row_scale)


def _gather_kernel(idx_ref, src_ref, o_ref, buf, sem, *, n_slots, n_rows):
    g, n_steps = pl.program_id(0), pl.num_programs(0)

    def issue(step, slot):
        def body(k, carry):
            for s in range(n_slots):
                row = idx_ref[s * n_rows + step * GATHER_ROWS + k]
                pltpu.make_async_copy(src_ref.at[pl.ds(row, 1)], buf.at[slot, pl.ds(s * GATHER_ROWS + k, 1)],
                                      sem.at[slot]).start()
            return carry
        lax.fori_loop(0, GATHER_ROWS, body, 0)

    @pl.when(g == 0)
    def _():
        issue(0, 0)

    @pl.when(g + 1 < n_steps)
    def _():
        issue(g + 1, (g + 1) % 2)

    slot = g % 2
    pltpu.make_async_copy(src_ref.at[pl.ds(0, n_slots * GATHER_ROWS)], buf.at[slot], sem.at[slot]).wait()
    acc = buf[slot, 0:GATHER_ROWS]
    for s in range(1, n_slots):
        acc = acc + buf[slot, s * GATHER_ROWS:(s + 1) * GATHER_ROWS]
    o_ref[...] = acc.astype(o_ref.dtype)


def _row_gather(src, idx, n_slots, out_dtype):
    n_rows = idx.shape[0] // n_slots
    return pl.pallas_call(
        functools.partial(_gather_kernel, n_slots=n_slots, n_rows=n_rows),
        grid_spec=pltpu.PrefetchScalarGridSpec(
            num_scalar_prefetch=1,
            grid=(n_rows // GATHER_ROWS,),
            in_specs=[pl.BlockSpec(memory_space=pl.ANY)],
            out_specs=pl.BlockSpec((GATHER_ROWS, D), lambda g, idx: (g, 0)),
            scratch_shapes=[pltpu.VMEM((2, n_slots * GATHER_ROWS, D), F32), pltpu.SemaphoreType.DMA((2,))],
        ),
        out_shape=jax.ShapeDtypeStruct((n_rows, D), out_dtype),
        compiler_params=_params(("arbitrary",), 32),
        name="row_gather",
    )(idx, src)


def _moe(hf, logits, w1, w3, w2, layer):
    lg = logits[:, :N_EXP]
    top_val, top_idx = lax.top_k(lg, 2)
    top_p = jax.nn.softmax(top_val, axis=-1)
    e_flat = top_idx.reshape(-1)
    onehot = (e_flat[:, None] == jnp.arange(N_EXP)[None, :]).astype(jnp.int32)
    rank = jnp.take_along_axis(jnp.cumsum(onehot, axis=0) - onehot, e_flat[:, None], axis=1)[:, 0]
    count = jnp.sum(onehot, axis=0)
    tiles_per = (count + MOE_TM - 1) // MOE_TM
    tile_end = jnp.cumsum(tiles_per)
    group_start = (tile_end - tiles_per) * MOE_TM
    pos = group_start[e_flat] + rank
    n_valid = tile_end[-1:].astype(jnp.int32)
    tile_expert = jnp.minimum(jnp.searchsorted(tile_end, jnp.arange(MOE_TILES), side="right"),
                              N_EXP - 1).astype(jnp.int32)
    rows = MOE_TILES * MOE_TM
    src = jnp.zeros((rows,), jnp.int32).at[pos].set(jnp.arange(2 * M, dtype=jnp.int32) // 2)
    scale = jnp.zeros((rows, 1), F32).at[pos, 0].set(top_p.reshape(-1))
    xs = _row_gather(hf, src, 1, BF)
    ys = _ffn(xs, w1, w3, w2, tile_expert + layer * N_EXP, n_valid, scale, tm=MOE_TM, tf=256)
    return _row_gather(ys, pos.reshape(M, 2).T.reshape(-1), 2, F32)


def _regroup_w_in(w):
    pad = jnp.zeros((D, 128), w.dtype)
    return jnp.concatenate([w[:, 0:2048], w[:, 2048:2304], w[:, 3328:3456], pad, w[:, 2304:3328], w[:, 3456:]],
                           axis=1).astype(BF)


def _rows_at(mat, off):
    return jnp.zeros((W, W), F32).at[off:off + mat.shape[0]].set(mat)


def _pad_rows(rows, n):
    z = jnp.stack(rows)
    return jnp.concatenate([z, jnp.zeros((n - z.shape[0],) + z.shape[1:], z.dtype)], axis=0)


def kernel(x, c, ctx, c_ctx, ada_w, ada_b, norm_mix, norm_ffn, w_in, na_rel_bias, rw_mu_kvr, rw_mu_lora, rw_w0, rw_w2, rw_a0, rw_a2, rw_k_k, rw_k_a, rw_r_k, rw_g2, rw_ln_w, rw_ln_b, sc_conv, w_branch, w_out, ffn_w1, ffn_w3, ffn_w2, moe_router, moe_w1, moe_w3, moe_w2, norm_final):
    xs = jnp.concatenate([ctx, x], axis=1)
    cvec = jnp.concatenate([c, c_ctx[None], jnp.zeros((3, D), F32)], axis=0)
    mods = _ada(cvec, ada_w, ada_b).reshape(DEPTH, 8, 6, D)

    def mod_table(gate, scale, shift):
        def pick(sel):
            if sel is None:
                return jnp.zeros((NB, 2, D), F32)
            v = mods[sel[0], :, sel[1]]
            return jnp.stack([jnp.broadcast_to(v[4], (NB, D)), v[:NB]], axis=1)
        t = jnp.stack([pick(gate), pick(scale), pick(shift)], axis=2)
        return jnp.concatenate([t, jnp.zeros((NB, 2, 5, D), F32)], axis=2)

    head_id = np.arange(W) // HD
    ones_bd = jnp.asarray((head_id[:, None] == head_id[None, :]).astype(np.float32))
    chan, cos, nsin, fscale = _dft_mats()
    chan, cos, nsin, fscale = jnp.asarray(chan, BF), jnp.asarray(cos, BF), jnp.asarray(nsin, BF), jnp.asarray(fscale)

    delta = None
    for l in range(DEPTH):
        mod = mod_table(None if l == 0 else (l - 1, 5), (l, 1), (l, 0))
        if l == 0:
            (h,) = _resid_norm(xs, None, mod, norm_mix[l])
        else:
            xs, h = _resid_norm(xs, delta, mod, norm_mix[l])
        u2 = _matmul(h.reshape(M, D), _regroup_w_in(w_in[l]), tm=512, tn=1536, out_dtype=F32, name="w_in")
        u3 = u2.reshape(NB, S, U_COLS)

        y_na = _attention(u3, _na_bias_table(na_rel_bias[l]))

        lora_mu = jnp.zeros((2, W), F32)
        lora_mu = lora_mu.at[:, LORA_W:LORA_W + 2 * LORA].set(
            jnp.broadcast_to(rw_mu_lora[l][:, 0].reshape(1, 2 * LORA), (2, 2 * LORA)))
        lora_mu = lora_mu.at[:, LORA_A:LORA_A + 2 * LORA].set(
            jnp.broadcast_to(rw_mu_lora[l][:, 1].reshape(1, 2 * LORA), (2, 2 * LORA)))
        vec = jnp.stack([_pad_rows([rw_mu_kvr[l][d, 0], rw_mu_kvr[l][d, 1], rw_mu_kvr[l][d, 2], lora_mu[d],
                                    rw_w0[l][d], rw_a0[l][d], rw_k_k[l], rw_k_a[l], rw_r_k[l]], 16)
                         for d in range(2)])
        w2e = jnp.stack([_rows_at(rw_w2[l][d], LORA_W + LORA * d) for d in range(2)])
        a2e = jnp.stack([_rows_at(rw_a2[l][d], LORA_A + LORA * d) for d in range(2)])
        seqs = [_rw_prep(u3, vec, w2e, a2e, ones_bd, d) for d in range(2)]
        ys = _rw_scan(seqs[0][:6], seqs[1][:6])
        bons = [seqs[0][6], seqs[1][6]]
        y_rw = _rw_out(ys[0], ys[1], bons[0], bons[1], u3, _rows_at(rw_g2[l], LORA_G).astype(BF),
                       ones_bd / HD, _pad_rows([rw_ln_w[l], rw_ln_b[l]], 8))

        y_fn = _fourier(u2, chan, cos, nsin, fscale)
        y_sc = _short_conv(u3, _pad_rows([sc_conv[l][0], sc_conv[l][1], sc_conv[l][2]], 8))

        merged = _merge([z.reshape(M, W) for z in (y_na, y_rw, y_fn, y_sc)], u2, w_branch[l].astype(BF))
        y = _matmul(merged, w_out[l].astype(BF), tm=512, tn=2048, out_dtype=F32, name="w_out")

        mod = mod_table((l, 2), (l, 4), (l, 3))
        i = l // 2
        if l % 2 == 0:
            xs, hf = _resid_norm(xs, y.reshape(NB, S, D), mod, norm_ffn[l])
            delta = _ffn(hf.reshape(M, D), ffn_w1, ffn_w3, ffn_w2,
                         jnp.full((M // 1024,), i, jnp.int32), jnp.full((1,), M // 1024, jnp.int32),
                         jnp.ones((M, 1), F32), tm=1024, tf=256)
        else:
            router = jnp.concatenate([moe_router[i], jnp.zeros((D, 128 - N_EXP), F32)], axis=1)
            xs, hf, logits = _resid_norm(xs, y.reshape(NB, S, D), mod, norm_ffn[l], router=router)
            n_moe = moe_w1.shape[0]
            delta = _moe(hf.reshape(M, D), logits.reshape(M, 128), moe_w1.reshape(n_moe * N_EXP, D, D_FF),
                         moe_w3.reshape(n_moe * N_EXP, D, D_FF), moe_w2.reshape(n_moe * N_EXP, D_FF, D), i)
        delta = delta.reshape(NB, S, D)

    mod = mod_table((DEPTH - 1, 5), None, None)
    (out,) = _resid_norm(xs, delta, mod, norm_final, final=True)
    return out
```

```python
import functools

import jax
import jax.numpy as jnp
import numpy as np
from jax import lax
from jax.experimental import pallas as pl
from jax.experimental.pallas import tpu as pltpu

BF = jnp.bfloat16
F32 = jnp.float32
HI = lax.Precision.HIGHEST

D = 2048
NB = 4
T = 2048
CTX = 256
S = CTX + T
M = NB * S
DEPTH = 4
GRID_W = 64
ROWS = T // GRID_W
WIN_H = 8
WIN_W = 16
HEADS = 8
HD = 64
W = HEADS * HD
LORA = 64
D_FF = 5632
N_EXP = 8
RMS_EPS = 1e-6
GN_EPS = 64e-5
NEG = -1e30

U_NA_K, U_NA_V, U_RW_K, U_RW_V, U_LORA, U_NA_Q, U_RW_R, U_FN_X, U_SC_B, U_SC_C, U_SC_X, U_GATES = range(12)
U_COLS = 27 * 512
LORA_W, LORA_A, LORA_G = 0, 128, 256

CHUNK = 64
N_CHUNK = S // CHUNK
CTX_CHUNKS = CTX // CHUNK
TOK_TILE = 256
MOE_TM = 1024
MOE_TILES = (2 * M) // MOE_TM + N_EXP
GATHER_ROWS = 256
FF_TILE = 512
HALO = 16


def _params(sem, vmem_mb):
    return pltpu.CompilerParams(dimension_semantics=sem, vmem_limit_bytes=vmem_mb << 20)


def _dot(a, b, **kw):
    return jnp.dot(a, b, preferred_element_type=F32, **kw)


def _dot_nt(a, b):
    return lax.dot_general(a, b, (((1,), (1,)), ((), ())), preferred_element_type=F32)


def _dot_tn(a, b):
    return lax.dot_general(a, b, (((0,), (0,)), ((), ())), preferred_element_type=F32)


def _head_sum(z, sel):
    hi = z.astype(BF)
    lo = (z - hi.astype(F32)).astype(BF)
    return _dot(hi, sel) + _dot(lo, sel)


def _ada_kernel(c_ref, w_ref, b_ref, o_ref):
    c = c_ref[...]
    a = (c * jax.nn.sigmoid(c)).astype(BF)
    o_ref[0] = _dot(a, w_ref[0].astype(BF)) + b_ref[0]


def _ada(cvec, ada_w, ada_b):
    tn = 1024
    return pl.pallas_call(
        _ada_kernel,
        grid=(DEPTH, 6 * D // tn),
        in_specs=[
            pl.BlockSpec((8, D), lambda l, j: (0, 0)),
            pl.BlockSpec((1, D, tn), lambda l, j: (l, 0, j)),
            pl.BlockSpec((1, 1, tn), lambda l, j: (l, 0, j)),
        ],
        out_specs=pl.BlockSpec((1, 8, tn), lambda l, j: (l, 0, j)),
        out_shape=jax.ShapeDtypeStruct((DEPTH, 8, 6 * D), F32),
        compiler_params=_params(("arbitrary", "arbitrary"), 40),
        name="ada",
    )(cvec, ada_w, ada_b.reshape(DEPTH, 1, 6 * D))


def _rn_kernel(*refs, has_delta, has_router, final):
    it = iter(refs)
    x_ref = next(it)
    d_ref = next(it) if has_delta else None
    mod_ref = next(it)
    g_ref = next(it)
    r_ref = next(it) if has_router else None
    xo_ref = next(it) if (has_delta and not final) else None
    h_ref = next(it)
    lg_ref = next(it) if has_router else None
    x = x_ref[0]
    mod = mod_ref[0, 0]
    if has_delta:
        x = x + mod[0:1] * d_ref[0]
        if xo_ref is not None:
            xo_ref[0] = x
    y = x * lax.rsqrt(jnp.mean(x * x, axis=-1, keepdims=True) + RMS_EPS) * g_ref[...]
    if final:
        h_ref[0] = y
        return
    h = y * (1.0 + mod[1:2]) + mod[2:3]
    h_ref[0] = h.astype(h_ref.dtype)
    if has_router:
        lg_ref[0] = _dot(h, r_ref[...], precision=HI)


def _resid_norm(x, delta, mod, g, router=None, final=False):
    has_delta = delta is not None
    has_router = router is not None
    n_t = S // TOK_TILE
    off = CTX // TOK_TILE if final else 0
    grid = (NB, n_t - off)
    tok = pl.BlockSpec((1, TOK_TILE, D), lambda b, i: (b, i + off, 0))
    in_specs = [tok] + ([tok] if has_delta else [])
    in_specs += [
        pl.BlockSpec((1, 1, 8, D), lambda b, i: (b, jnp.minimum(i + off, 1), 0, 0)),
        pl.BlockSpec((1, D), lambda b, i: (0, 0)),
    ]
    args = [x] + ([delta] if has_delta else []) + [mod, g.reshape(1, D)]
    out_specs, out_shape = [], []
    if final:
        out_specs.append(pl.BlockSpec((1, TOK_TILE, D), lambda b, i: (b, i, 0)))
        out_shape.append(jax.ShapeDtypeStruct((NB, T, D), F32))
    else:
        if has_delta:
            out_specs.append(tok)
            out_shape.append(jax.ShapeDtypeStruct((NB, S, D), F32))
        out_specs.append(tok)
        out_shape.append(jax.ShapeDtypeStruct((NB, S, D), F32 if has_router else BF))
    if has_router:
        in_specs.append(pl.BlockSpec((D, 128), lambda b, i: (0, 0)))
        args.append(router)
        out_specs.append(pl.BlockSpec((1, TOK_TILE, 128), lambda b, i: (b, i, 0)))
        out_shape.append(jax.ShapeDtypeStruct((NB, S, 128), F32))
    return pl.pallas_call(
        functools.partial(_rn_kernel, has_delta=has_delta, has_router=has_router, final=final),
        grid=grid,
        in_specs=in_specs,
        out_specs=out_specs,
        out_shape=out_shape,
        compiler_params=_params(("arbitrary", "arbitrary"), 40),
        name="resid_norm",
    )(*args)


def _mm_kernel(a_ref, w_ref, o_ref):
    o_ref[...] = _dot(a_ref[...].astype(BF), w_ref[...].astype(BF)).astype(o_ref.dtype)


def _matmul(a, w, *, tm, tn, out_dtype, a_col_blk=0, k=None, name="matmul"):
    m = a.shape[0]
    k = k or a.shape[1]
    n = w.shape[1]
    return pl.pallas_call(
        _mm_kernel,
        grid=(n // tn, m // tm),
        in_specs=[
            pl.BlockSpec((tm, k), lambda j, i: (i, a_col_blk)),
            pl.BlockSpec((k, tn), lambda j, i: (0, j)),
        ],
        out_specs=pl.BlockSpec((tm, tn), lambda j, i: (i, j)),
        out_shape=jax.ShapeDtypeStruct((m, n), out_dtype),
        compiler_params=_params(("arbitrary", "arbitrary"), 48),
        name=name,
    )(a, w)


def _na_window(r):
    rr = r - CTX // GRID_W
    rs = jnp.clip(rr - WIN_H // 2, 0, ROWS - WIN_H)
    return rr, rs


def _na_kernel(q_ref, k_ref, v_ref, bias_ref, o_ref):
    _, rs = _na_window(pl.program_id(1))
    kstart = pl.multiple_of(CTX + GRID_W * rs, GRID_W)
    lo = lax.broadcasted_iota(jnp.int32, (GRID_W, 128), 1) < HD
    n_nb = WIN_H * GRID_W
    for p in range(HEADS // 2):
        cs = slice(128 * p, 128 * p + 128)
        q = q_ref[0, :, cs].astype(F32) * (HD ** -0.5)
        q2 = jnp.concatenate([jnp.where(lo, q, 0.0), jnp.where(lo, 0.0, q)], axis=0).astype(BF)
        kn = k_ref[0, pl.ds(kstart, n_nb), cs].astype(BF)
        kc = k_ref[0, 0:CTX, cs].astype(BF)
        s_nb = _dot_nt(q2, kn) + bias_ref[0, 128 * p:128 * p + 128, :]
        s_cx = _dot_nt(q2, kc)
        m = jnp.maximum(jnp.max(s_nb, axis=-1, keepdims=True), jnp.max(s_cx, axis=-1, keepdims=True))
        e_nb = jnp.exp(s_nb - m)
        e_cx = jnp.exp(s_cx - m)
        l = jnp.sum(e_nb, axis=-1, keepdims=True) + jnp.sum(e_cx, axis=-1, keepdims=True)
        vn = v_ref[0, pl.ds(kstart, n_nb), cs].astype(BF)
        vc = v_ref[0, 0:CTX, cs].astype(BF)
        o2 = (_dot(e_nb.astype(BF), vn) + _dot(e_cx.astype(BF), vc)) / l
        o_ref[0, :, cs] = jnp.where(lo, o2[0:GRID_W], o2[GRID_W:]).astype(o_ref.dtype)


def _na_bias_table(rel_bias):
    cols = np.arange(GRID_W)
    col_start = np.clip(cols - WIN_W // 2, 0, GRID_W - WIN_W)
    col_ok = (cols[None, :] >= col_start[:, None]) & (cols[None, :] < col_start[:, None] + WIN_W)
    dc = np.clip(cols[None, :] - cols[:, None] + (WIN_W - 1), 0, 2 * WIN_W - 2)
    pick = (dc[None] == np.arange(2 * WIN_W - 1)[:, None, None]).astype(np.float32)
    by_dr = jnp.einsum("hac,cqk->haqk", rel_bias.astype(F32), pick, precision=HI)
    by_dr = jnp.where(col_ok[None, None], by_dr, NEG)
    tab = jnp.stack([by_dr[:, d0:d0 + WIN_H] for d0 in range(WIN_H)])
    tab = jnp.transpose(tab, (0, 1, 3, 2, 4)).reshape(WIN_H, HEADS * GRID_W, WIN_H * GRID_W)
    return jnp.concatenate([tab, jnp.full((1,) + tab.shape[1:], NEG, F32)], axis=0)


def _attention(u3, bias_tab):
    def bias_idx(b, r):
        rr, rs = _na_window(r)
        return (jnp.where(rr < 0, WIN_H, rs - rr + WIN_H - 1), 0, 0)

    return pl.pallas_call(
        _na_kernel,
        grid=(NB, S // GRID_W),
        in_specs=[
            pl.BlockSpec((1, GRID_W, W), lambda b, r: (b, r, U_NA_Q)),
            pl.BlockSpec((1, S, W), lambda b, r: (b, 0, U_NA_K)),
            pl.BlockSpec((1, S, W), lambda b, r: (b, 0, U_NA_V)),
            pl.BlockSpec((1, HEADS * GRID_W, WIN_H * GRID_W), bias_idx),
        ],
        out_specs=pl.BlockSpec((1, GRID_W, W), lambda b, r: (b, r, 0)),
        out_shape=jax.ShapeDtypeStruct((NB, S, W), BF),
        compiler_params=_params(("arbitrary", "arbitrary"), 48),
        name="attention",
    )(u3, u3, u3, bias_tab)


def _shift_prev(z, halo_row, first_is_start, rev):
    n = z.shape[0]
    row = lax.broadcasted_iota(jnp.int32, z.shape, 0)
    edge = jnp.where(first_is_start, 0.0, halo_row)
    if rev:
        return jnp.where(row == n - 1, edge, pltpu.roll(z, n - 1, 0))
    return jnp.where(row == 0, edge, pltpu.roll(z, 1, 0))


def _rw_prep_kernel(k_ref, v_ref, r_ref, l_ref, kh_ref, vh_ref, rh_ref, lh_ref, vec_ref, w2_ref, a2_ref,
                    ones_ref, lw_o, k_o, v_o, kk_o, b_o, r_o, bonus_o, *, rev):
    i = pl.program_id(1)
    n_t = S // TOK_TILE
    start = (i == n_t - 1) | (i == 0) if rev else (i <= 1)
    hrow = 0 if rev else HALO - 1
    vec = vec_ref[0]
    mu_k, mu_v, mu_r, mu_l, w0, a0, k_k, k_a, r_k = (vec[j:j + 1] for j in range(9))

    def lerp(ref, href, mu):
        z = ref[0].astype(F32)
        prev = _shift_prev(z, href[0].astype(F32)[hrow:hrow + 1, :], start, rev)
        return z + (prev - z) * mu

    lz = lerp(l_ref, lh_ref, mu_l)
    wz = w0 + _dot(jnp.tanh(lz[:, LORA_W:LORA_W + 128]), w2_ref[0], precision=HI)
    a = jax.nn.sigmoid(a0 + _dot(lz[:, LORA_A:LORA_A + 128], a2_ref[0], precision=HI))
    softplus = jnp.maximum(-wz, 0.0) + jnp.log1p(jnp.exp(-jnp.abs(wz)))
    lw_o[0] = -jnp.exp(-softplus - 0.5)
    k1 = lerp(k_ref, kh_ref, mu_k)
    kkr = k1 * k_k
    ones = ones_ref[...]
    nrm = jnp.sqrt(_head_sum(kkr * kkr, ones))
    kk = kkr / jnp.maximum(nrm, 1e-12)
    k2 = k1 * (1.0 + (a - 1.0) * k_a)
    v1 = lerp(v_ref, vh_ref, mu_v)
    r1 = lerp(r_ref, rh_ref, mu_r)
    k_o[0] = k2
    v_o[0] = v1
    kk_o[0] = kk
    b_o[0] = kk * a
    r_o[0] = r1
    bonus_o[0] = _head_sum(r1 * k2 * r_k, ones) * v1


def _rw_prep(u3, vec, w2e, a2e, ones_bd, d):
    rev = d == 1
    n_t = S // TOK_TILE
    hb = TOK_TILE // HALO

    def tile(blk):
        return pl.BlockSpec((1, TOK_TILE, W), lambda b, i: (b, i, blk))

    def halo(blk):
        if rev:
            return pl.BlockSpec((1, HALO, W), lambda b, i: (b, jnp.minimum((i + 1) * hb, S // HALO - 1), blk))
        return pl.BlockSpec((1, HALO, W), lambda b, i: (b, jnp.maximum(i * hb - 1, 0), blk))

    blks = (U_RW_K, U_RW_V, U_RW_R, U_LORA)
    out = pl.BlockSpec((1, TOK_TILE, W), lambda b, i: (b, i, 0))
    return pl.pallas_call(
        functools.partial(_rw_prep_kernel, rev=rev),
        grid=(NB, n_t),
        in_specs=[tile(x) for x in blks] + [halo(x) for x in blks] + [
            pl.BlockSpec((1, 16, W), lambda b, i: (d, 0, 0)),
            pl.BlockSpec((1, 2 * LORA, W), lambda b, i: (d, 0, 0)),
            pl.BlockSpec((1, 2 * LORA, W), lambda b, i: (d, 0, 0)),
            pl.BlockSpec((W, W), lambda b, i: (0, 0)),
        ],
        out_specs=[out] * 7,
        out_shape=[jax.ShapeDtypeStruct((NB, S, W), F32)] * 7,
        compiler_params=_params(("arbitrary", "arbitrary"), 48),
        name="rw_prep",
    )(*([u3] * 8), vec, w2e, a2e, ones_bd)


def _pair_rows(z, lo):
    return jnp.concatenate([jnp.where(lo, z, 0.0), jnp.where(lo, 0.0, z)], axis=0)


def _rw_scan_kernel(*refs):
    ins, y_refs, s_ref = (refs[0:6], refs[6:12]), refs[12:14], refs[14]

    @pl.when(pl.program_id(1) == 0)
    def _():
        s_ref[...] = jnp.zeros_like(s_ref)

    n = CHUNK
    t_i = lax.broadcasted_iota(jnp.int32, (n, n), 0)
    s_i = lax.broadcasted_iota(jnp.int32, (n, n), 1)
    t2 = lax.broadcasted_iota(jnp.int32, (2 * n, 2 * n), 0)
    s2 = lax.broadcasted_iota(jnp.int32, (2 * n, 2 * n), 1)
    same = (t2 >= n) == (s2 >= n)
    tt, ss = t2 & (n - 1), s2 & (n - 1)
    eye2 = t2 == s2
    lo = lax.broadcasted_iota(jnp.int32, (n, 128), 1) < HD
    incl1 = (s_i <= t_i, s_i >= t_i)
    incl2 = (same & (ss <= tt), same & (ss >= tt))
    strict2 = (same & (ss < tt), same & (ss > tt))
    last = (n - 1, 0)
    bf = lambda z: z.astype(BF)

    lw_all = [ins[d][0][0] for d in range(2)]
    cum_all = [_dot(incl1[d].astype(F32), lw_all[d], precision=HI) for d in range(2)]
    chains = [(d, p) for d in range(2) for p in range(HEADS // 2)]

    def operands(d, p):
        cs = slice(128 * p, 128 * p + 128)
        _, k_ref, v_ref, kk_ref, b_ref, r_ref = ins[d]
        lw, cum = lw_all[d][:, cs], cum_all[d][:, cs]
        tot = cum[last[d]:last[d] + 1, :]
        g, gp, gi, gc = jnp.exp(cum), jnp.exp(cum - lw), jnp.exp(-cum), jnp.exp(tot - cum)
        kk, b, k, r, v = kk_ref[0, :, cs], b_ref[0, :, cs], k_ref[0, :, cs], r_ref[0, :, cs], v_ref[0, :, cs]
        return dict(
            a2=bf(_pair_rows(-kk * gp, lo)), r2=_pair_rows(r * g, lo), b2=bf(_pair_rows(b * gi, lo)),
            k2=bf(_pair_rows(k * gi, lo)), bt2=bf(_pair_rows(b * gc, lo)), kt2=bf(_pair_rows(k * gc, lo)),
            v2=bf(_pair_rows(v, lo)), gtot=jnp.exp(tot))

    op = [operands(d, p) for d, p in chains]
    sc = [_dot_nt(jnp.concatenate([o["a2"], bf(o["r2"])], axis=0), jnp.concatenate([o["b2"], o["k2"]], axis=0))
          for o in op]
    m_ab = [jnp.where(strict2[d], s[:2 * n, :2 * n], 0.0) for (d, _), s in zip(chains, sc)]
    m_ak = [bf(jnp.where(strict2[d], s[:2 * n, 2 * n:], 0.0)) for (d, _), s in zip(chains, sc)]
    m_rb = [bf(jnp.where(incl2[d], s[2 * n:, :2 * n], 0.0)) for (d, _), s in zip(chains, sc)]
    m_rk = [bf(jnp.where(incl2[d], s[2 * n:, 2 * n:], 0.0)) for (d, _), s in zip(chains, sc)]
    x = [jnp.where(eye2, 1.0, 0.0) + m for m in m_ab]
    pw = m_ab
    for _ in range(int(np.log2(n)) - 1):
        pw = [_dot(bf(z), bf(z)) for z in pw]
        x = [xi + _dot(bf(xi), bf(z)) for xi, z in zip(x, pw)]
    xb = [bf(z) for z in x]
    mv = [bf(_dot(m, o["v2"])) for m, o in zip(m_ak, op)]
    wmb = [bf(_dot(z, o["a2"])) for z, o in zip(xb, op)]
    ub = [bf(_dot(z, m)) for z, m in zip(xb, mv)]
    rm = [o["r2"] + _dot(m, w) for o, m, w in zip(op, m_rb, wmb)]
    yu = [_dot(mb, u) + _dot(mk, o["v2"]) for mb, mk, u, o in zip(m_rb, m_rk, ub, op)]
    tm = [jnp.where(eye2, o["gtot"], 0.0) + _dot_tn(o["bt2"], w) for o, w in zip(op, wmb)]
    sv = [_dot_tn(o["bt2"], u) + _dot_tn(o["kt2"], o["v2"]) for o, u in zip(op, ub)]
    st = [bf(s_ref[d, p]) for d, p in chains]
    y2 = [_dot(bf(r), s) + y for r, s, y in zip(rm, st, yu)]
    s_new = [_dot(bf(t), s) + v for t, s, v in zip(tm, st, sv)]
    for (d, p), y, s in zip(chains, y2, s_new):
        y_refs[d][0, :, 128 * p:128 * p + 128] = y[:n] + y[n:]
        s_ref[d, p] = s


def _rw_scan(fwd, rev):
    def rev_idx(b, c):
        return (b, jnp.where(c < CTX_CHUNKS, CTX_CHUNKS - 1 - c, N_CHUNK + CTX_CHUNKS - 1 - c), 0)

    f_spec = pl.BlockSpec((1, CHUNK, W), lambda b, c: (b, c, 0))
    r_spec = pl.BlockSpec((1, CHUNK, W), rev_idx)
    return pl.pallas_call(
        _rw_scan_kernel,
        grid=(NB, N_CHUNK),
        in_specs=[f_spec] * 6 + [r_spec] * 6,
        out_specs=[f_spec, r_spec],
        out_shape=[jax.ShapeDtypeStruct((NB, S, W), F32)] * 2,
        scratch_shapes=[pltpu.VMEM((2, HEADS // 2, 128, 128), F32)],
        compiler_params=_params(("arbitrary", "arbitrary"), 32),
        name="rw_scan",
    )(*fwd, *rev)


def _rw_out_kernel(y0_ref, y1_ref, b0_ref, b1_ref, l_ref, g2_ref, avg_ref, ln_ref, o_ref):
    y = y0_ref[0] + y1_ref[0]
    avg = avg_ref[...]
    mu = _head_sum(y, avg)
    yc = y - mu
    var = _head_sum(yc * yc, avg)
    ln = ln_ref[...]
    y = yc * lax.rsqrt(var + GN_EPS) * ln[0:1] + ln[1:2] + b0_ref[0] + b1_ref[0]
    gate = _dot(jax.nn.sigmoid(l_ref[0].astype(F32)).astype(BF), g2_ref[...])
    o_ref[0] = (y * gate).astype(o_ref.dtype)


def _rw_out(y0, y1, bon0, bon1, u3, g2e, avg_bd, ln):
    tile = pl.BlockSpec((1, TOK_TILE, W), lambda b, i: (b, i, 0))
    return pl.pallas_call(
        _rw_out_kernel,
        grid=(NB, S // TOK_TILE),
        in_specs=[tile] * 4 + [
            pl.BlockSpec((1, TOK_TILE, W), lambda b, i: (b, i, U_LORA)),
            pl.BlockSpec((W, W), lambda b, i: (0, 0)),
            pl.BlockSpec((W, W), lambda b, i: (0, 0)),
            pl.BlockSpec((8, W), lambda b, i: (0, 0)),
        ],
        out_specs=tile,
        out_shape=jax.ShapeDtypeStruct((NB, S, W), BF),
        compiler_params=_params(("arbitrary", "arbitrary"), 32),
        name="rw_out",
    )(y0, y1, bon0, bon1, u3, g2e, avg_bd, ln)


def _dft_mats():
    gd = W // 4
    cc = np.arange(gd)
    ang = 2 * np.pi * ((cc[:, None] * cc[None, :]) % gd) / gd
    chan = np.zeros((W, 2 * W), np.float32)
    for g in range(4):
        chan[g * gd:(g + 1) * gd, g * gd:(g + 1) * gd] = np.cos(ang)
        chan[g * gd:(g + 1) * gd, W + g * gd:W + (g + 1) * gd] = np.sin(ang)
    cos = np.zeros((S, S), np.float32)
    nsin = np.zeros((S, S), np.float32)
    scale = np.zeros((S, 1), np.float32)
    for lo, n in ((0, CTX), (CTX, T)):
        tt = np.arange(n)
        a = 2 * np.pi * ((tt[:, None] * tt[None, :]) % n) / n
        cos[lo:lo + n, lo:lo + n] = np.cos(a)
        nsin[lo:lo + n, lo:lo + n] = -np.sin(a)
        scale[lo:lo + n] = (n * gd) ** -0.5
    return chan, cos, nsin, scale


def _fn_tok_kernel(cos_ref, nsin_ref, z_ref, sc_ref, o_ref):
    z = z_ref[0]
    f = _dot(cos_ref[...], z[:, 0:W]) + _dot(nsin_ref[...], z[:, W:2 * W])
    o_ref[0] = (f * sc_ref[...]).astype(o_ref.dtype)


def _fourier(u2, chan, cos, nsin, scale):
    zcs = _matmul(u2, chan, tm=1024, tn=2 * W, out_dtype=BF, a_col_blk=U_FN_X, k=W, name="fn_chan")
    return pl.pallas_call(
        _fn_tok_kernel,
        grid=(NB, S // TOK_TILE),
        in_specs=[
            pl.BlockSpec((TOK_TILE, S), lambda b, i: (i, 0)),
            pl.BlockSpec((TOK_TILE, S), lambda b, i: (i, 0)),
            pl.BlockSpec((1, S, 2 * W), lambda b, i: (b, 0, 0)),
            pl.BlockSpec((TOK_TILE, 1), lambda b, i: (i, 0)),
        ],
        out_specs=pl.BlockSpec((1, TOK_TILE, W), lambda b, i: (b, i, 0)),
        out_shape=jax.ShapeDtypeStruct((NB, S, W), BF),
        compiler_params=_params(("arbitrary", "arbitrary"), 40),
        name="fn_tok",
    )(cos, nsin, zcs.reshape(NB, S, 2 * W), scale)


def _sc_kernel(b_ref, c_ref, x_ref, cp_ref, xp_ref, cn_ref, xn_ref, w_ref, o_ref):
    i = pl.program_id(1)
    n_t = S // TOK_TILE
    f32 = lambda ref: ref[0].astype(F32)
    z = f32(c_ref) * f32(x_ref)
    zp = _shift_prev(z, (f32(cp_ref) * f32(xp_ref))[HALO - 1:HALO, :], i <= 1, False)
    zn = _shift_prev(z, (f32(cn_ref) * f32(xn_ref))[0:1, :], (i == 0) | (i == n_t - 1), True)
    w = w_ref[...]
    o_ref[0] = (f32(b_ref) * (zp * w[0:1] + z * w[1:2] + zn * w[2:3])).astype(o_ref.dtype)


def _short_conv(u3, conv_w):
    hb = TOK_TILE // HALO

    def tile(blk):
        return pl.BlockSpec((1, TOK_TILE, W), lambda b, i: (b, i, blk))

    def prev(blk):
        return pl.BlockSpec((1, HALO, W), lambda b, i: (b, jnp.maximum(i * hb - 1, 0), blk))

    def nxt(blk):
        return pl.BlockSpec((1, HALO, W), lambda b, i: (b, jnp.minimum((i + 1) * hb, S // HALO - 1), blk))

    return pl.pallas_call(
        _sc_kernel,
        grid=(NB, S // TOK_TILE),
        in_specs=[tile(U_SC_B), tile(U_SC_C), tile(U_SC_X), prev(U_SC_C), prev(U_SC_X), nxt(U_SC_C), nxt(U_SC_X),
                  pl.BlockSpec((8, W), lambda b, i: (0, 0))],
        out_specs=pl.BlockSpec((1, TOK_TILE, W), lambda b, i: (b, i, 0)),
        out_shape=jax.ShapeDtypeStruct((NB, S, W), BF),
        compiler_params=_params(("arbitrary", "arbitrary"), 32),
        name="short_conv",
    )(*([u3] * 7), conv_w)


def _merge_kernel(y0_ref, y1_ref, y2_ref, y3_ref, g0_ref, g1_ref, g2_ref, g3_ref, w_ref, o_ref):
    acc = None
    for n, (y_ref, g_ref) in enumerate(((y0_ref, g0_ref), (y1_ref, g1_ref), (y2_ref, g2_ref), (y3_ref, g3_ref))):
        t = jax.nn.sigmoid(g_ref[...].astype(F32)) * _dot(y_ref[...], w_ref[n])
        acc = t if acc is None else acc + t
    o_ref[...] = acc.astype(o_ref.dtype)


def _merge(branches, u2, w_branch):
    tm, tn = 512, 512

    def gate(n):
        return pl.BlockSpec((tm, tn), lambda i, j: (i, U_GATES + n * (D // tn) + j))

    return pl.pallas_call(
        _merge_kernel,
        grid=(M // tm, D // tn),
        in_specs=[pl.BlockSpec((tm, W), lambda i, j: (i, 0))] * 4 + [gate(n) for n in range(4)] + [
            pl.BlockSpec((4, W, tn), lambda i, j: (0, 0, j))],
        out_specs=pl.BlockSpec((tm, tn), lambda i, j: (i, j)),
        out_shape=jax.ShapeDtypeStruct((M, D), BF),
        compiler_params=_params(("arbitrary", "arbitrary"), 32),
        name="merge",
    )(*branches, u2, u2, u2, u2, w_branch)


def _ffn_kernel(te_ref, nv_ref, x_ref, w1_ref, w3_ref, w2_ref, s_ref, o_ref):
    i, f = pl.program_id(0), pl.program_id(1)

    @pl.when(f == 0)
    def _():
        o_ref[...] = jnp.zeros_like(o_ref)

    @pl.when(i < nv_ref[0])
    def _():
        x = x_ref[...]
        a = _dot(x, w1_ref[0].astype(BF))
        b = _dot(x, w3_ref[0].astype(BF))
        g = (a * jax.nn.sigmoid(a) * b).astype(BF)
        o_ref[...] += _dot(g, w2_ref[0].astype(BF))

        @pl.when(f == pl.num_programs(1) - 1)
        def _():
            o_ref[...] *= s_ref[...]


def _ffn(x, w1, w3, w2, tile_expert, n_valid, row_scale, *, tm, tf):
    rows = x.shape[0]
    n_tiles = rows // tm

    n_f = D_FF // tf

    def tile(i, nv):
        return jnp.minimum(i, nv[0] - 1)

    def ff(i, f, nv):
        return jnp.where(i < nv[0], f, n_f - 1)

    once = dict(pipeline_mode=pl.Buffered(1))
    return pl.pallas_call(
        _ffn_kernel,
        grid_spec=pltpu.PrefetchScalarGridSpec(
            num_scalar_prefetch=2,
            grid=(n_tiles, n_f),
            in_specs=[
                pl.BlockSpec((tm, D), lambda i, f, te, nv: (tile(i, nv), 0), **once),
                pl.BlockSpec((1, D, tf), lambda i, f, te, nv: (te[tile(i, nv)], 0, ff(i, f, nv))),
                pl.BlockSpec((1, D, tf), lambda i, f, te, nv: (te[tile(i, nv)], 0, ff(i, f, nv))),
                pl.BlockSpec((1, tf, D), lambda i, f, te, nv: (te[tile(i, nv)], ff(i, f, nv), 0)),
                pl.BlockSpec((tm, 1), lambda i, f, te, nv: (tile(i, nv), 0), **once),
            ],
            out_specs=pl.BlockSpec((tm, D), lambda i, f, te, nv: (i, 0), **once),
        ),
        out_shape=jax.ShapeDtypeStruct((rows, D), F32),
        compiler_params=_params(("arbitrary", "arbitrary"), 56),
        name="ffn",
    )(tile_expert, n_valid, x, w1, w3, w2, row_scale)


def _gather_kernel(idx_ref, live_ref, src_ref, o_ref, buf, sem, *, n_slots, n_rows):
    g, n_live = pl.program_id(0), live_ref[0]

    def issue(step, slot):
        def body(k, carry):
            for s in range(n_slots):
                row = idx_ref[s * n_rows + step * GATHER_ROWS + k]
                pltpu.make_async_copy(src_ref.at[pl.ds(row, 1)], buf.at[slot, pl.ds(s * GATHER_ROWS + k, 1)],
                                      sem.at[slot]).start()
            return carry
        lax.fori_loop(0, GATHER_ROWS, body, 0, unroll=8)

    @pl.when(g == 0)
    def _():
        issue(0, 0)

    @pl.when(g + 1 < n_live)
    def _():
        issue(g + 1, (g + 1) % 2)

    @pl.when(g < n_live)
    def _():
        slot = g % 2
        pltpu.make_async_copy(src_ref.at[pl.ds(0, n_slots * GATHER_ROWS)], buf.at[slot], sem.at[slot]).wait()
        acc = buf[slot, 0:GATHER_ROWS]
        for s in range(1, n_slots):
            acc = acc + buf[slot, s * GATHER_ROWS:(s + 1) * GATHER_ROWS]
        o_ref[...] = acc.astype(o_ref.dtype)

    @pl.when(g >= n_live)
    def _():
        o_ref[...] = jnp.zeros_like(o_ref)


def _row_gather(src, idx, n_slots, out_dtype, n_live_rows=None):
    n_rows = idx.shape[0] // n_slots
    n_steps = n_rows // GATHER_ROWS
    live = jnp.full((1,), n_steps, jnp.int32) if n_live_rows is None else n_live_rows // GATHER_ROWS
    return pl.pallas_call(
        functools.partial(_gather_kernel, n_slots=n_slots, n_rows=n_rows),
        grid_spec=pltpu.PrefetchScalarGridSpec(
            num_scalar_prefetch=2,
            grid=(n_steps,),
            in_specs=[pl.BlockSpec(memory_space=pl.ANY)],
            out_specs=pl.BlockSpec((GATHER_ROWS, D), lambda g, idx, live: (g, 0)),
            scratch_shapes=[pltpu.VMEM((2, n_slots * GATHER_ROWS, D), F32), pltpu.SemaphoreType.DMA((2,))],
        ),
        out_shape=jax.ShapeDtypeStruct((n_rows, D), out_dtype),
        compiler_params=_params(("arbitrary",), 32),
        name="row_gather",
    )(idx, live, src)


def _moe(hf, logits, w1, w3, w2, layer):
    lg = logits[:, :N_EXP]
    top_val, top_idx = lax.top_k(lg, 2)
    top_p = jax.nn.softmax(top_val, axis=-1)
    e_flat = top_idx.reshape(-1)
    onehot = (e_flat[:, None] == jnp.arange(N_EXP)[None, :]).astype(jnp.int32)
    rank = jnp.take_along_axis(jnp.cumsum(onehot, axis=0) - onehot, e_flat[:, None], axis=1)[:, 0]
    count = jnp.sum(onehot, axis=0)
    tiles_per = (count + MOE_TM - 1) // MOE_TM
    tile_end = jnp.cumsum(tiles_per)
    group_start = (tile_end - tiles_per) * MOE_TM
    pos = group_start[e_flat] + rank
    n_valid = tile_end[-1:].astype(jnp.int32)
    tile_expert = jnp.minimum(jnp.searchsorted(tile_end, jnp.arange(MOE_TILES), side="right"),
                              N_EXP - 1).astype(jnp.int32)
    rows = MOE_TILES * MOE_TM
    src = jnp.zeros((rows,), jnp.int32).at[pos].set(jnp.arange(2 * M, dtype=jnp.int32) // 2)
    scale = jnp.zeros((rows, 1), F32).at[pos, 0].set(top_p.reshape(-1))
    xs = _row_gather(hf, src, 1, BF, n_live_rows=n_valid * MOE_TM)
    ys = _ffn(xs, w1, w3, w2, tile_expert + layer * N_EXP, n_valid, scale, tm=MOE_TM, tf=FF_TILE)
    return _row_gather(ys, pos.reshape(M, 2).T.reshape(-1), 2, F32)


def _regroup_w_in(w):
    pad = jnp.zeros((D, 128), w.dtype)
    return jnp.concatenate([w[:, 0:2048], w[:, 2048:2304], w[:, 3328:3456], pad, w[:, 2304:3328], w[:, 3456:]],
                           axis=1).astype(BF)


def _rows_at(mat, off, n=W):
    return jnp.zeros((n, W), F32).at[off:off + mat.shape[0]].set(mat)


def _pad_rows(rows, n):
    z = jnp.stack(rows)
    return jnp.concatenate([z, jnp.zeros((n - z.shape[0],) + z.shape[1:], z.dtype)], axis=0)


def kernel(x, c, ctx, c_ctx, ada_w, ada_b, norm_mix, norm_ffn, w_in, na_rel_bias, rw_mu_kvr, rw_mu_lora, rw_w0, rw_w2, rw_a0, rw_a2, rw_k_k, rw_k_a, rw_r_k, rw_g2, rw_ln_w, rw_ln_b, sc_conv, w_branch, w_out, ffn_w1, ffn_w3, ffn_w2, moe_router, moe_w1, moe_w3, moe_w2, norm_final):
    xs = jnp.concatenate([ctx, x], axis=1)
    cvec = jnp.concatenate([c, c_ctx[None], jnp.zeros((3, D), F32)], axis=0)
    mods = _ada(cvec, ada_w, ada_b).reshape(DEPTH, 8, 6, D)

    def mod_table(gate, scale, shift):
        def pick(sel):
            if sel is None:
                return jnp.zeros((NB, 2, D), F32)
            v = mods[sel[0], :, sel[1]]
            return jnp.stack([jnp.broadcast_to(v[4], (NB, D)), v[:NB]], axis=1)
        t = jnp.stack([pick(gate), pick(scale), pick(shift)], axis=2)
        return jnp.concatenate([t, jnp.zeros((NB, 2, 5, D), F32)], axis=2)

    head_id = np.arange(W) // HD
    ones_bd = jnp.asarray((head_id[:, None] == head_id[None, :]).astype(np.float32), BF)
    chan, cos, nsin, fscale = _dft_mats()
    chan, cos, nsin, fscale = jnp.asarray(chan, BF), jnp.asarray(cos, BF), jnp.asarray(nsin, BF), jnp.asarray(fscale)

    delta = None
    for l in range(DEPTH):
        mod = mod_table(None if l == 0 else (l - 1, 5), (l, 1), (l, 0))
        if l == 0:
            (h,) = _resid_norm(xs, None, mod, norm_mix[l])
        else:
            xs, h = _resid_norm(xs, delta, mod, norm_mix[l])
        u2 = _matmul(h.reshape(M, D), _regroup_w_in(w_in[l]), tm=512, tn=1536, out_dtype=BF, name="w_in")
        u3 = u2.reshape(NB, S, U_COLS)

        y_na = _attention(u3, _na_bias_table(na_rel_bias[l]))

        lora_mu = jnp.zeros((2, W), F32)
        lora_mu = lora_mu.at[:, LORA_W:LORA_W + 2 * LORA].set(
            jnp.broadcast_to(rw_mu_lora[l][:, 0].reshape(1, 2 * LORA), (2, 2 * LORA)))
        lora_mu = lora_mu.at[:, LORA_A:LORA_A + 2 * LORA].set(
            jnp.broadcast_to(rw_mu_lora[l][:, 1].reshape(1, 2 * LORA), (2, 2 * LORA)))
        vec = jnp.stack([_pad_rows([rw_mu_kvr[l][d, 0], rw_mu_kvr[l][d, 1], rw_mu_kvr[l][d, 2], lora_mu[d],
                                    rw_w0[l][d], rw_a0[l][d], rw_k_k[l], rw_k_a[l], rw_r_k[l]], 16)
                         for d in range(2)])
        w2e = jnp.stack([_rows_at(rw_w2[l][d], LORA * d, 2 * LORA) for d in range(2)])
        a2e = jnp.stack([_rows_at(rw_a2[l][d], LORA * d, 2 * LORA) for d in range(2)])
        seqs = [_rw_prep(u3, vec, w2e, a2e, ones_bd, d) for d in range(2)]
        ys = _rw_scan(seqs[0][:6], seqs[1][:6])
        bons = [seqs[0][6], seqs[1][6]]
        y_rw = _rw_out(ys[0], ys[1], bons[0], bons[1], u3, _rows_at(rw_g2[l], LORA_G).astype(BF),
                       ones_bd / HD, _pad_rows([rw_ln_w[l], rw_ln_b[l]], 8))

        y_fn = _fourier(u2, chan, cos, nsin, fscale)
        y_sc = _short_conv(u3, _pad_rows([sc_conv[l][0], sc_conv[l][1], sc_conv[l][2]], 8))

        merged = _merge([z.reshape(M, W) for z in (y_na, y_rw, y_fn, y_sc)], u2, w_branch[l].astype(BF))
        y = _matmul(merged, w_out[l].astype(BF), tm=512, tn=2048, out_dtype=F32, name="w_out")

        mod = mod_table((l, 2), (l, 4), (l, 3))
        i = l // 2
        if l % 2 == 0:
            xs, hf = _resid_norm(xs, y.reshape(NB, S, D), mod, norm_ffn[l])
            delta = _ffn(hf.reshape(M, D), ffn_w1, ffn_w3, ffn_w2,
                         jnp.full((M // 1024,), i, jnp.int32), jnp.full((1,), M // 1024, jnp.int32),
                         jnp.ones((M, 1), F32), tm=1024, tf=FF_TILE)
        else:
            router = jnp.concatenate([moe_router[i], jnp.zeros((D, 128 - N_EXP), F32)], axis=1)
            xs, hf, logits = _resid_norm(xs, y.reshape(NB, S, D), mod, norm_ffn[l], router=router)
            n_moe = moe_w1.shape[0]
            delta = _moe(hf.reshape(M, D), logits.reshape(M, 128), moe_w1.reshape(n_moe * N_EXP, D, D_FF),
                         moe_w3.reshape(n_moe * N_EXP, D, D_FF), moe_w2.reshape(n_moe * N_EXP, D_FF, D), i)
        delta = delta.reshape(NB, S, D)

    mod = mod_table((DEPTH - 1, 5), None, None)
    (out,) = _resid_norm(xs, delta, mod, norm_final, final=True)
    return out
```

```python
import functools

import jax
import jax.numpy as jnp
import numpy as np
from jax import lax
from jax.experimental import pallas as pl
from jax.experimental.pallas import tpu as pltpu

BF = jnp.bfloat16
F32 = jnp.float32
HI = lax.Precision.HIGHEST

D = 2048
NB = 4
T = 2048
CTX = 256
S = CTX + T
M = NB * S
DEPTH = 4
GRID_W = 64
ROWS = T // GRID_W
WIN_H = 8
WIN_W = 16
HEADS = 8
HD = 64
W = HEADS * HD
LORA = 64
D_FF = 5632
N_EXP = 8
RMS_EPS = 1e-6
GN_EPS = 64e-5
NEG = -1e30

U_NA_K, U_NA_V, U_RW_K, U_RW_V, U_NA_Q, U_RW_R, U_FN_X, U_SC_B, U_SC_C, U_SC_X, U_GATES = range(11)
W_IN_TILE = 1024
W_IN_SRC = (0, 1024, 2304, 3456, 4480) + tuple(5504 + W_IN_TILE * t for t in range(8))
U_COLS = W_IN_TILE * len(W_IN_SRC)
LORA_W, LORA_A, LORA_G = 0, 128, 256

CHUNK = 64
N_CHUNK = S // CHUNK
CTX_CHUNKS = CTX // CHUNK
TOK_TILE = 256
MOE_TM = 1024
GATHER_ROWS = 256
FF_TILE = 512
HALO = 16


def _params(sem, vmem_mb):
    return pltpu.CompilerParams(dimension_semantics=sem, vmem_limit_bytes=vmem_mb << 20)


def _dot(a, b, **kw):
    return jnp.dot(a, b, preferred_element_type=F32, **kw)


def _dot_nt(a, b):
    return lax.dot_general(a, b, (((1,), (1,)), ((), ())), preferred_element_type=F32)


def _dot_tn(a, b):
    return lax.dot_general(a, b, (((0,), (0,)), ((), ())), preferred_element_type=F32)


def _head_sum(z, sel):
    hi = z.astype(BF)
    lo = (z - hi.astype(F32)).astype(BF)
    return _dot(hi, sel) + _dot(lo, sel)


def _ada_kernel(c_ref, w_ref, b_ref, o_ref):
    c = c_ref[...]
    a = (c * jax.nn.sigmoid(c)).astype(BF)
    o_ref[0] = _dot(a, w_ref[0].astype(BF)) + b_ref[0]


def _ada(cvec, ada_w, ada_b):
    tn = 1024
    return pl.pallas_call(
        _ada_kernel,
        grid=(DEPTH, 6 * D // tn),
        in_specs=[
            pl.BlockSpec((8, D), lambda l, j: (0, 0)),
            pl.BlockSpec((1, D, tn), lambda l, j: (l, 0, j)),
            pl.BlockSpec((1, 1, tn), lambda l, j: (l, 0, j)),
        ],
        out_specs=pl.BlockSpec((1, 8, tn), lambda l, j: (l, 0, j)),
        out_shape=jax.ShapeDtypeStruct((DEPTH, 8, 6 * D), F32),
        compiler_params=_params(("arbitrary", "arbitrary"), 40),
        name="ada",
    )(cvec, ada_w, ada_b.reshape(DEPTH, 1, 6 * D))


def _rn_kernel(*refs, has_delta, has_proj, has_router, final):
    it = iter(refs)
    x_ref = next(it)
    d_ref = next(it) if has_delta else None
    p_ref = next(it) if has_proj else None
    mod_ref = next(it)
    g_ref = next(it)
    r_ref = next(it) if has_router else None
    xo_ref = next(it) if (has_delta and not final) else None
    h_ref = next(it)
    lg_ref = next(it) if has_router else None
    x = x_ref[0]
    mod = mod_ref[0, 0]
    if has_delta:
        d = _dot(d_ref[0], p_ref[...]) if has_proj else d_ref[0]
        x = x + mod[0:1] * d
        if xo_ref is not None:
            xo_ref[0] = x
    y = x * lax.rsqrt(jnp.mean(x * x, axis=-1, keepdims=True) + RMS_EPS) * g_ref[...]
    if final:
        h_ref[0] = y
        return
    h = y * (1.0 + mod[1:2]) + mod[2:3]
    h_ref[0] = h.astype(h_ref.dtype)
    if has_router:
        lg_ref[0] = _dot(h, r_ref[...], precision=HI)


def _resid_norm(x, delta, mod, g, router=None, final=False, proj=None):
    has_delta = delta is not None
    has_router = router is not None
    has_proj = proj is not None
    n_t = S // TOK_TILE
    off = CTX // TOK_TILE if final else 0
    grid = (NB, n_t - off)
    tok = pl.BlockSpec((1, TOK_TILE, D), lambda b, i: (b, i + off, 0))
    in_specs = [tok] + ([pl.BlockSpec((1, TOK_TILE, D), lambda b, i: (b, i, 0))] if has_delta else [])
    in_specs += [pl.BlockSpec((D, D), lambda b, i: (0, 0), pipeline_mode=pl.Buffered(1))] if has_proj else []
    in_specs += [
        pl.BlockSpec((1, 1, 8, D), lambda b, i: (b, jnp.minimum(i + off, 1), 0, 0)),
        pl.BlockSpec((1, D), lambda b, i: (0, 0)),
    ]
    args = [x] + ([delta] if has_delta else []) + ([proj] if has_proj else []) + [mod, g.reshape(1, D)]
    out_specs, out_shape = [], []
    if final:
        out_specs.append(pl.BlockSpec((1, TOK_TILE, D), lambda b, i: (b, i, 0)))
        out_shape.append(jax.ShapeDtypeStruct((NB, T, D), F32))
    else:
        if has_delta:
            out_specs.append(tok)
            out_shape.append(jax.ShapeDtypeStruct((NB, S, D), F32))
        out_specs.append(tok)
        out_shape.append(jax.ShapeDtypeStruct((NB, S, D), F32 if has_router else BF))
    if has_router:
        in_specs.append(pl.BlockSpec((D, 128), lambda b, i: (0, 0)))
        args.append(router)
        out_specs.append(pl.BlockSpec((1, TOK_TILE, 128), lambda b, i: (b, i, 0)))
        out_shape.append(jax.ShapeDtypeStruct((NB, S, 128), F32))
    return pl.pallas_call(
        functools.partial(_rn_kernel, has_delta=has_delta, has_proj=has_proj, has_router=has_router, final=final),
        grid=grid,
        in_specs=in_specs,
        out_specs=out_specs,
        out_shape=out_shape,
        compiler_params=_params(("arbitrary", "arbitrary"), 40),
        name="resid_norm",
    )(*args)


def _mm_kernel(a_ref, w_ref, o_ref):
    o_ref[...] = _dot(a_ref[...].astype(BF), w_ref[...].astype(BF)).astype(o_ref.dtype)


def _matmul(a, w, *, tm, tn, out_dtype, a_col_blk=0, k=None, name="matmul"):
    m = a.shape[0]
    k = k or a.shape[1]
    n = w.shape[1]
    return pl.pallas_call(
        _mm_kernel,
        grid=(n // tn, m // tm),
        in_specs=[
            pl.BlockSpec((tm, k), lambda j, i: (i, a_col_blk)),
            pl.BlockSpec((k, tn), lambda j, i: (0, j)),
        ],
        out_specs=pl.BlockSpec((tm, tn), lambda j, i: (i, j)),
        out_shape=jax.ShapeDtypeStruct((m, n), out_dtype),
        compiler_params=_params(("arbitrary", "arbitrary"), 48),
        name=name,
    )(a, w)


def _w_in_kernel(off_ref, a_ref, w_ref, o_ref, wb_ref):
    @pl.when(pl.program_id(1) == 0)
    def _():
        wb_ref[...] = w_ref[...].astype(BF)

    o_ref[...] = _dot(a_ref[...], wb_ref[...]).astype(o_ref.dtype)


def _w_in(h, w):
    tm = 512
    return pl.pallas_call(
        _w_in_kernel,
        grid_spec=pltpu.PrefetchScalarGridSpec(
            num_scalar_prefetch=1,
            grid=(len(W_IN_SRC), M // tm),
            in_specs=[
                pl.BlockSpec((tm, D), lambda j, i, off: (i, 0)),
                pl.BlockSpec((pl.Element(D), pl.Element(W_IN_TILE)),
                             lambda j, i, off: (0, pl.multiple_of(off[j], 128))),
            ],
            out_specs=pl.BlockSpec((tm, W_IN_TILE), lambda j, i, off: (i, j)),
            scratch_shapes=[pltpu.VMEM((D, W_IN_TILE), BF)],
        ),
        out_shape=jax.ShapeDtypeStruct((M, U_COLS), BF),
        compiler_params=_params(("arbitrary", "arbitrary"), 48),
        name="w_in",
    )(jnp.asarray(W_IN_SRC, jnp.int32), h, w)


def _na_window(r):
    rr = r - CTX // GRID_W
    rs = jnp.clip(rr - WIN_H // 2, 0, ROWS - WIN_H)
    return rr, rs


def _na_kernel(q_ref, k_ref, v_ref, bias_ref, o_ref):
    _, rs = _na_window(pl.program_id(1))
    kstart = pl.multiple_of(CTX + GRID_W * rs, GRID_W)
    lo = lax.broadcasted_iota(jnp.int32, (GRID_W, 128), 1) < HD
    n_nb = WIN_H * GRID_W
    pairs = [slice(128 * p, 128 * p + 128) for p in range(HEADS // 2)]
    rowmax = lambda z: jnp.max(z, axis=-1, keepdims=True)
    rowsum = lambda z: jnp.sum(z, axis=-1, keepdims=True)
    q = [q_ref[0, :, cs].astype(F32) * (HD ** -0.5) for cs in pairs]
    q2 = [jnp.concatenate([jnp.where(lo, z, 0.0), jnp.where(lo, 0.0, z)], axis=0).astype(BF) for z in q]
    s_nb = [_dot_nt(z, k_ref[0, pl.ds(kstart, n_nb), cs]) + bias_ref[0, 128 * p:128 * p + 128, :]
            for p, (z, cs) in enumerate(zip(q2, pairs))]
    s_cx = [_dot_nt(z, k_ref[0, 0:CTX, cs]) for z, cs in zip(q2, pairs)]
    m = [jnp.maximum(rowmax(a), rowmax(b)) for a, b in zip(s_nb, s_cx)]
    e_nb = [jnp.exp(a - mi) for a, mi in zip(s_nb, m)]
    e_cx = [jnp.exp(b - mi) for b, mi in zip(s_cx, m)]
    l = [rowsum(a) + rowsum(b) for a, b in zip(e_nb, e_cx)]
    o2 = [(_dot(a.astype(BF), v_ref[0, pl.ds(kstart, n_nb), cs]) + _dot(b.astype(BF), v_ref[0, 0:CTX, cs])) / li
          for a, b, li, cs in zip(e_nb, e_cx, l, pairs)]
    for z, cs in zip(o2, pairs):
        o_ref[0, :, cs] = jnp.where(lo, z[0:GRID_W], z[GRID_W:]).astype(o_ref.dtype)


def _na_bias_table(rel_bias):
    cols = np.arange(GRID_W)
    col_start = np.clip(cols - WIN_W // 2, 0, GRID_W - WIN_W)
    col_ok = (cols[None, :] >= col_start[:, None]) & (cols[None, :] < col_start[:, None] + WIN_W)
    dc = np.clip(cols[None, :] - cols[:, None] + (WIN_W - 1), 0, 2 * WIN_W - 2)
    pick = (dc[None] == np.arange(2 * WIN_W - 1)[:, None, None]).astype(np.float32)
    by_dr = jnp.einsum("hac,cqk->haqk", rel_bias.astype(F32), pick, precision=HI)
    by_dr = jnp.where(col_ok[None, None], by_dr, NEG)
    tab = jnp.stack([by_dr[:, d0:d0 + WIN_H] for d0 in range(WIN_H)])
    tab = jnp.transpose(tab, (0, 1, 3, 2, 4)).reshape(WIN_H, HEADS * GRID_W, WIN_H * GRID_W)
    return jnp.concatenate([tab, jnp.full((1,) + tab.shape[1:], NEG, F32)], axis=0)


def _attention(u3, bias_tab):
    def bias_idx(b, r):
        rr, rs = _na_window(r)
        return (jnp.where(rr < 0, WIN_H, rs - rr + WIN_H - 1), 0, 0)

    return pl.pallas_call(
        _na_kernel,
        grid=(NB, S // GRID_W),
        in_specs=[
            pl.BlockSpec((1, GRID_W, W), lambda b, r: (b, r, U_NA_Q)),
            pl.BlockSpec((1, S, W), lambda b, r: (b, 0, U_NA_K)),
            pl.BlockSpec((1, S, W), lambda b, r: (b, 0, U_NA_V)),
            pl.BlockSpec((1, HEADS * GRID_W, WIN_H * GRID_W), bias_idx),
        ],
        out_specs=pl.BlockSpec((1, GRID_W, W), lambda b, r: (b, r, 0)),
        out_shape=jax.ShapeDtypeStruct((NB, S, W), BF),
        compiler_params=_params(("arbitrary", "arbitrary"), 48),
        name="attention",
    )(u3, u3, u3, bias_tab)


def _shift_prev(z, halo_row, first_is_start, rev):
    n = z.shape[0]
    row = lax.broadcasted_iota(jnp.int32, z.shape, 0)
    edge = jnp.where(first_is_start, 0.0, halo_row)
    if rev:
        return jnp.where(row == n - 1, edge, pltpu.roll(z, n - 1, 0))
    return jnp.where(row == 0, edge, pltpu.roll(z, 1, 0))


def _rw_prep_kernel(k_ref, v_ref, r_ref, l_ref, kh_ref, vh_ref, rh_ref, lh_ref, vec_ref, w2_ref, a2_ref,
                    ones_ref, lw_o, k_o, v_o, kk_o, b_o, r_o, bonus_o, *, rev):
    i = pl.program_id(1)
    n_t = S // TOK_TILE
    start = (i == n_t - 1) | (i == 0) if rev else (i <= 1)
    hrow = 0 if rev else HALO - 1
    vec = vec_ref[0]
    mu_k, mu_v, mu_r, mu_l, w0, a0, k_k, k_a, r_k = (vec[j:j + 1] for j in range(9))

    def lerp(ref, href, mu):
        z = ref[0].astype(F32)
        prev = _shift_prev(z, href[0].astype(F32)[hrow:hrow + 1, :], start, rev)
        return z + (prev - z) * mu

    lz = lerp(l_ref, lh_ref, mu_l)
    wz = w0 + _dot(jnp.tanh(lz[:, LORA_W:LORA_W + 128]), w2_ref[0], precision=HI)
    a = jax.nn.sigmoid(a0 + _dot(lz[:, LORA_A:LORA_A + 128], a2_ref[0], precision=HI))
    softplus = jnp.maximum(-wz, 0.0) + jnp.log1p(jnp.exp(-jnp.abs(wz)))
    lw_o[0] = -jnp.exp(-softplus - 0.5)
    k1 = lerp(k_ref, kh_ref, mu_k)
    kkr = k1 * k_k
    ones = ones_ref[...]
    nrm = jnp.sqrt(_head_sum(kkr * kkr, ones))
    kk = kkr / jnp.maximum(nrm, 1e-12)
    k2 = k1 * (1.0 + (a - 1.0) * k_a)
    v1 = lerp(v_ref, vh_ref, mu_v)
    r1 = lerp(r_ref, rh_ref, mu_r)
    k_o[0] = k2
    v_o[0] = v1
    kk_o[0] = kk
    b_o[0] = kk * a
    r_o[0] = r1
    bonus_o[0] = _head_sum(r1 * k2 * r_k, ones) * v1


def _rw_prep(u3, ul3, vec, w2e, a2e, ones_bd, d):
    rev = d == 1
    n_t = S // TOK_TILE
    hb = TOK_TILE // HALO

    def tile(blk):
        return pl.BlockSpec((1, TOK_TILE, W), lambda b, i: (b, i, blk))

    def halo(blk):
        if rev:
            return pl.BlockSpec((1, HALO, W), lambda b, i: (b, jnp.minimum((i + 1) * hb, S // HALO - 1), blk))
        return pl.BlockSpec((1, HALO, W), lambda b, i: (b, jnp.maximum(i * hb - 1, 0), blk))

    blks = (U_RW_K, U_RW_V, U_RW_R, 0)
    out = pl.BlockSpec((1, TOK_TILE, W), lambda b, i: (b, i, 0))
    return pl.pallas_call(
        functools.partial(_rw_prep_kernel, rev=rev),
        grid=(NB, n_t),
        in_specs=[tile(x) for x in blks] + [halo(x) for x in blks] + [
            pl.BlockSpec((1, 16, W), lambda b, i: (d, 0, 0)),
            pl.BlockSpec((1, 2 * LORA, W), lambda b, i: (d, 0, 0)),
            pl.BlockSpec((1, 2 * LORA, W), lambda b, i: (d, 0, 0)),
            pl.BlockSpec((W, W), lambda b, i: (0, 0)),
        ],
        out_specs=[out] * 7,
        out_shape=[jax.ShapeDtypeStruct((NB, S, W), F32)] * 7,
        compiler_params=_params(("arbitrary", "arbitrary"), 48),
        name="rw_prep",
    )(u3, u3, u3, ul3, u3, u3, u3, ul3, vec, w2e, a2e, ones_bd)


def _pair_rows(z, lo):
    return jnp.concatenate([jnp.where(lo, z, 0.0), jnp.where(lo, 0.0, z)], axis=0)


def _rw_scan_kernel(*refs):
    ins, y_refs, s_ref = (refs[0:6], refs[6:12]), refs[12:14], refs[14]

    @pl.when(pl.program_id(1) == 0)
    def _():
        s_ref[...] = jnp.zeros_like(s_ref)

    n = CHUNK
    t_i = lax.broadcasted_iota(jnp.int32, (n, n), 0)
    s_i = lax.broadcasted_iota(jnp.int32, (n, n), 1)
    t2 = lax.broadcasted_iota(jnp.int32, (2 * n, 2 * n), 0)
    s2 = lax.broadcasted_iota(jnp.int32, (2 * n, 2 * n), 1)
    same = (t2 >= n) == (s2 >= n)
    tt, ss = t2 & (n - 1), s2 & (n - 1)
    eye2 = t2 == s2
    lo = lax.broadcasted_iota(jnp.int32, (n, 128), 1) < HD
    incl1 = (s_i <= t_i, s_i >= t_i)
    incl2 = (same & (ss <= tt), same & (ss >= tt))
    strict2 = (same & (ss < tt), same & (ss > tt))
    last = (n - 1, 0)
    bf = lambda z: z.astype(BF)

    lw_all = [ins[d][0][0] for d in range(2)]
    cum_all = [_dot(incl1[d].astype(F32), lw_all[d], precision=HI) for d in range(2)]
    chains = [(d, p) for d in range(2) for p in range(HEADS // 2)]

    def operands(d, p):
        cs = slice(128 * p, 128 * p + 128)
        _, k_ref, v_ref, kk_ref, b_ref, r_ref = ins[d]
        lw, cum = lw_all[d][:, cs], cum_all[d][:, cs]
        tot = cum[last[d]:last[d] + 1, :]
        g, gp, gi, gc = jnp.exp(cum), jnp.exp(cum - lw), jnp.exp(-cum), jnp.exp(tot - cum)
        kk, b, k, r, v = kk_ref[0, :, cs], b_ref[0, :, cs], k_ref[0, :, cs], r_ref[0, :, cs], v_ref[0, :, cs]
        return dict(
            a2=bf(_pair_rows(-kk * gp, lo)), r2=_pair_rows(r * g, lo), b2=bf(_pair_rows(b * gi, lo)),
            k2=bf(_pair_rows(k * gi, lo)), bt2=bf(_pair_rows(b * gc, lo)), kt2=bf(_pair_rows(k * gc, lo)),
            v2=bf(_pair_rows(v, lo)), gtot=jnp.exp(tot))

    op = [operands(d, p) for d, p in chains]
    sc = [_dot_nt(jnp.concatenate([o["a2"], bf(o["r2"])], axis=0), jnp.concatenate([o["b2"], o["k2"]], axis=0))
          for o in op]
    m_ab = [jnp.where(strict2[d], s[:2 * n, :2 * n], 0.0) for (d, _), s in zip(chains, sc)]
    m_ak = [bf(jnp.where(strict2[d], s[:2 * n, 2 * n:], 0.0)) for (d, _), s in zip(chains, sc)]
    m_rb = [bf(jnp.where(incl2[d], s[2 * n:, :2 * n], 0.0)) for (d, _), s in zip(chains, sc)]
    m_rk = [bf(jnp.where(incl2[d], s[2 * n:, 2 * n:], 0.0)) for (d, _), s in zip(chains, sc)]
    x = [jnp.where(eye2, 1.0, 0.0) + m for m in m_ab]
    pw = m_ab
    for _ in range(int(np.log2(n)) - 1):
        pw = [_dot(bf(z), bf(z)) for z in pw]
        x = [xi + _dot(bf(xi), bf(z)) for xi, z in zip(x, pw)]
    xb = [bf(z) for z in x]
    mv = [bf(_dot(m, o["v2"])) for m, o in zip(m_ak, op)]
    wmb = [bf(_dot(z, o["a2"])) for z, o in zip(xb, op)]
    ub = [bf(_dot(z, m)) for z, m in zip(xb, mv)]
    rm = [o["r2"] + _dot(m, w) for o, m, w in zip(op, m_rb, wmb)]
    yu = [_dot(mb, u) + _dot(mk, o["v2"]) for mb, mk, u, o in zip(m_rb, m_rk, ub, op)]
    tm = [jnp.where(eye2, o["gtot"], 0.0) + _dot_tn(o["bt2"], w) for o, w in zip(op, wmb)]
    sv = [_dot_tn(o["bt2"], u) + _dot_tn(o["kt2"], o["v2"]) for o, u in zip(op, ub)]
    st = [bf(s_ref[d, p]) for d, p in chains]
    y2 = [_dot(bf(r), s) + y for r, s, y in zip(rm, st, yu)]
    s_new = [_dot(bf(t), s) + v for t, s, v in zip(tm, st, sv)]
    for (d, p), y, s in zip(chains, y2, s_new):
        y_refs[d][0, :, 128 * p:128 * p + 128] = y[:n] + y[n:]
        s_ref[d, p] = s


def _rw_scan(fwd, rev):
    def rev_idx(b, c):
        return (b, jnp.where(c < CTX_CHUNKS, CTX_CHUNKS - 1 - c, N_CHUNK + CTX_CHUNKS - 1 - c), 0)

    f_spec = pl.BlockSpec((1, CHUNK, W), lambda b, c: (b, c, 0))
    r_spec = pl.BlockSpec((1, CHUNK, W), rev_idx)
    return pl.pallas_call(
        _rw_scan_kernel,
        grid=(NB, N_CHUNK),
        in_specs=[f_spec] * 6 + [r_spec] * 6,
        out_specs=[f_spec, r_spec],
        out_shape=[jax.ShapeDtypeStruct((NB, S, W), F32)] * 2,
        scratch_shapes=[pltpu.VMEM((2, HEADS // 2, 128, 128), F32)],
        compiler_params=_params(("arbitrary", "arbitrary"), 32),
        name="rw_scan",
    )(*fwd, *rev)


def _rw_out_kernel(y0_ref, y1_ref, b0_ref, b1_ref, l_ref, g2_ref, avg_ref, ln_ref, o_ref):
    y = y0_ref[0] + y1_ref[0]
    avg = avg_ref[...]
    mu = _head_sum(y, avg)
    yc = y - mu
    var = _head_sum(yc * yc, avg)
    ln = ln_ref[...]
    y = yc * lax.rsqrt(var + GN_EPS) * ln[0:1] + ln[1:2] + b0_ref[0] + b1_ref[0]
    gate = _dot(jax.nn.sigmoid(l_ref[0].astype(F32)).astype(BF), g2_ref[...])
    o_ref[0] = (y * gate).astype(o_ref.dtype)


def _rw_out(y0, y1, bon0, bon1, ul3, g2e, avg_bd, ln):
    tile = pl.BlockSpec((1, TOK_TILE, W), lambda b, i: (b, i, 0))
    return pl.pallas_call(
        _rw_out_kernel,
        grid=(NB, S // TOK_TILE),
        in_specs=[tile] * 4 + [
            tile,
            pl.BlockSpec((W, W), lambda b, i: (0, 0)),
            pl.BlockSpec((W, W), lambda b, i: (0, 0)),
            pl.BlockSpec((8, W), lambda b, i: (0, 0)),
        ],
        out_specs=tile,
        out_shape=jax.ShapeDtypeStruct((NB, S, W), BF),
        compiler_params=_params(("arbitrary", "arbitrary"), 32),
        name="rw_out",
    )(y0, y1, bon0, bon1, ul3, g2e, avg_bd, ln)


def _dft_mats():
    gd = W // 4
    cc = np.arange(gd)
    ang = 2 * np.pi * ((cc[:, None] * cc[None, :]) % gd) / gd
    chan = np.zeros((W, 2 * W), np.float32)
    for g in range(4):
        chan[g * gd:(g + 1) * gd, g * gd:(g + 1) * gd] = np.cos(ang)
        chan[g * gd:(g + 1) * gd, W + g * gd:W + (g + 1) * gd] = np.sin(ang)
    cos = np.zeros((S, S), np.float32)
    nsin = np.zeros((S, S), np.float32)
    scale = np.zeros((S, 1), np.float32)
    for lo, n in ((0, CTX), (CTX, T)):
        tt = np.arange(n)
        a = 2 * np.pi * ((tt[:, None] * tt[None, :]) % n) / n
        cos[lo:lo + n, lo:lo + n] = np.cos(a)
        nsin[lo:lo + n, lo:lo + n] = -np.sin(a)
        scale[lo:lo + n] = (n * gd) ** -0.5
    return chan, cos, nsin, scale


def _fn_tok_kernel(cos_ref, nsin_ref, z_ref, sc_ref, o_ref):
    z = z_ref[0]
    f = _dot(cos_ref[...], z[:, 0:W]) + _dot(nsin_ref[...], z[:, W:2 * W])
    o_ref[0] = (f * sc_ref[...]).astype(o_ref.dtype)


def _fourier(u2, chan, cos, nsin, scale):
    zcs = _matmul(u2, chan, tm=1024, tn=2 * W, out_dtype=BF, a_col_blk=U_FN_X, k=W, name="fn_chan")
    return pl.pallas_call(
        _fn_tok_kernel,
        grid=(NB, S // TOK_TILE),
        in_specs=[
            pl.BlockSpec((TOK_TILE, S), lambda b, i: (i, 0)),
            pl.BlockSpec((TOK_TILE, S), lambda b, i: (i, 0)),
            pl.BlockSpec((1, S, 2 * W), lambda b, i: (b, 0, 0)),
            pl.BlockSpec((TOK_TILE, 1), lambda b, i: (i, 0)),
        ],
        out_specs=pl.BlockSpec((1, TOK_TILE, W), lambda b, i: (b, i, 0)),
        out_shape=jax.ShapeDtypeStruct((NB, S, W), BF),
        compiler_params=_params(("arbitrary", "arbitrary"), 40),
        name="fn_tok",
    )(cos, nsin, zcs.reshape(NB, S, 2 * W), scale)


def _sc_kernel(b_ref, c_ref, x_ref, cp_ref, xp_ref, cn_ref, xn_ref, w_ref, o_ref):
    i = pl.program_id(1)
    n_t = S // TOK_TILE
    f32 = lambda ref: ref[0].astype(F32)
    z = f32(c_ref) * f32(x_ref)
    zp = _shift_prev(z, (f32(cp_ref) * f32(xp_ref))[HALO - 1:HALO, :], i <= 1, False)
    zn = _shift_prev(z, (f32(cn_ref) * f32(xn_ref))[0:1, :], (i == 0) | (i == n_t - 1), True)
    w = w_ref[...]
    o_ref[0] = (f32(b_ref) * (zp * w[0:1] + z * w[1:2] + zn * w[2:3])).astype(o_ref.dtype)


def _short_conv(u3, conv_w):
    hb = TOK_TILE // HALO

    def tile(blk):
        return pl.BlockSpec((1, TOK_TILE, W), lambda b, i: (b, i, blk))

    def prev(blk):
        return pl.BlockSpec((1, HALO, W), lambda b, i: (b, jnp.maximum(i * hb - 1, 0), blk))

    def nxt(blk):
        return pl.BlockSpec((1, HALO, W), lambda b, i: (b, jnp.minimum((i + 1) * hb, S // HALO - 1), blk))

    return pl.pallas_call(
        _sc_kernel,
        grid=(NB, S // TOK_TILE),
        in_specs=[tile(U_SC_B), tile(U_SC_C), tile(U_SC_X), prev(U_SC_C), prev(U_SC_X), nxt(U_SC_C), nxt(U_SC_X),
                  pl.BlockSpec((8, W), lambda b, i: (0, 0))],
        out_specs=pl.BlockSpec((1, TOK_TILE, W), lambda b, i: (b, i, 0)),
        out_shape=jax.ShapeDtypeStruct((NB, S, W), BF),
        compiler_params=_params(("arbitrary", "arbitrary"), 32),
        name="short_conv",
    )(*([u3] * 7), conv_w)


def _merge_kernel(y0_ref, y1_ref, y2_ref, y3_ref, g0_ref, g1_ref, g2_ref, g3_ref, w_ref, o_ref):
    acc = None
    for n, (y_ref, g_ref) in enumerate(((y0_ref, g0_ref), (y1_ref, g1_ref), (y2_ref, g2_ref), (y3_ref, g3_ref))):
        t = jax.nn.sigmoid(g_ref[...].astype(F32)) * _dot(y_ref[...], w_ref[n])
        acc = t if acc is None else acc + t
    o_ref[...] = acc.astype(o_ref.dtype)


def _merge(branches, u2, w_branch):
    tm, tn = 512, 512

    def gate(n):
        return pl.BlockSpec((tm, tn), lambda i, j: (i, U_GATES + n * (D // tn) + j))

    return pl.pallas_call(
        _merge_kernel,
        grid=(M // tm, D // tn),
        in_specs=[pl.BlockSpec((tm, W), lambda i, j: (i, 0))] * 4 + [gate(n) for n in range(4)] + [
            pl.BlockSpec((4, W, tn), lambda i, j: (0, 0, j))],
        out_specs=pl.BlockSpec((tm, tn), lambda i, j: (i, j)),
        out_shape=jax.ShapeDtypeStruct((M, D), BF),
        compiler_params=_params(("arbitrary", "arbitrary"), 32),
        name="merge",
    )(*branches, u2, u2, u2, u2, w_branch)


def _ffn_kernel(te_ref, nv_ref, x_ref, w1_ref, w3_ref, w2_ref, s_ref, o_ref):
    i, f = pl.program_id(0), pl.program_id(1)

    @pl.when(f == 0)
    def _():
        o_ref[...] = jnp.zeros_like(o_ref)

    @pl.when(i < nv_ref[0])
    def _():
        x = x_ref[...]
        a = _dot(x, w1_ref[0].astype(BF))
        b = _dot(x, w3_ref[0].astype(BF))
        g = (a * jax.nn.sigmoid(a) * b).astype(BF)
        o_ref[...] += _dot(g, w2_ref[0].astype(BF))

        @pl.when(f == pl.num_programs(1) - 1)
        def _():
            o_ref[...] *= s_ref[...]


def _ffn(x, w1, w3, w2, tile_expert, n_valid, row_scale, *, tm, tf):
    rows = x.shape[0]
    n_tiles = rows // tm

    n_f = D_FF // tf

    def tile(i, nv):
        return jnp.minimum(i, nv[0] - 1)

    def ff(i, f, nv):
        return jnp.where(i < nv[0], f, n_f - 1)

    once = dict(pipeline_mode=pl.Buffered(1))
    return pl.pallas_call(
        _ffn_kernel,
        grid_spec=pltpu.PrefetchScalarGridSpec(
            num_scalar_prefetch=2,
            grid=(n_tiles, n_f),
            in_specs=[
                pl.BlockSpec((tm, D), lambda i, f, te, nv: (tile(i, nv), 0), **once),
                pl.BlockSpec((1, D, tf), lambda i, f, te, nv: (te[tile(i, nv)], 0, ff(i, f, nv))),
                pl.BlockSpec((1, D, tf), lambda i, f, te, nv: (te[tile(i, nv)], 0, ff(i, f, nv))),
                pl.BlockSpec((1, tf, D), lambda i, f, te, nv: (te[tile(i, nv)], ff(i, f, nv), 0)),
                pl.BlockSpec((tm, 1), lambda i, f, te, nv: (tile(i, nv), 0), **once),
            ],
            out_specs=pl.BlockSpec((tm, D), lambda i, f, te, nv: (i, 0), **once),
        ),
        out_shape=jax.ShapeDtypeStruct((rows, D), F32),
        compiler_params=_params(("arbitrary", "arbitrary"), 56),
        name="ffn",
    )(tile_expert, n_valid, x, w1, w3, w2, row_scale)


def _gather_kernel(idx_ref, live_ref, src_ref, o_ref, buf, sem, *, n_slots, n_rows):
    g, n_live = pl.program_id(0), live_ref[0]

    def issue(step, slot):
        def body(k, carry):
            for s in range(n_slots):
                row = idx_ref[s * n_rows + step * GATHER_ROWS + k]
                pltpu.make_async_copy(src_ref.at[pl.ds(row, 1)], buf.at[slot, pl.ds(s * GATHER_ROWS + k, 1)],
                                      sem.at[slot]).start()
            return carry
        lax.fori_loop(0, GATHER_ROWS, body, 0, unroll=8)

    @pl.when(g == 0)
    def _():
        issue(0, 0)

    @pl.when(g + 1 < n_live)
    def _():
        issue(g + 1, (g + 1) % 2)

    @pl.when(g < n_live)
    def _():
        slot = g % 2
        pltpu.make_async_copy(src_ref.at[pl.ds(0, n_slots * GATHER_ROWS)], buf.at[slot], sem.at[slot]).wait()
        acc = buf[slot, 0:GATHER_ROWS]
        for s in range(1, n_slots):
            acc = acc + buf[slot, s * GATHER_ROWS:(s + 1) * GATHER_ROWS]
        o_ref[...] = acc.astype(o_ref.dtype)

    @pl.when(g >= n_live)
    def _():
        o_ref[...] = jnp.zeros_like(o_ref)


def _row_gather(src, idx, n_slots, out_dtype, n_live_rows=None):
    n_rows = idx.shape[0] // n_slots
    n_steps = n_rows // GATHER_ROWS
    live = jnp.full((1,), n_steps, jnp.int32) if n_live_rows is None else n_live_rows // GATHER_ROWS
    return pl.pallas_call(
        functools.partial(_gather_kernel, n_slots=n_slots, n_rows=n_rows),
        grid_spec=pltpu.PrefetchScalarGridSpec(
            num_scalar_prefetch=2,
            grid=(n_steps,),
            in_specs=[pl.BlockSpec(memory_space=pl.ANY)],
            out_specs=pl.BlockSpec((GATHER_ROWS, D), lambda g, idx, live: (g, 0)),
            scratch_shapes=[pltpu.VMEM((2, n_slots * GATHER_ROWS, D), F32), pltpu.SemaphoreType.DMA((2,))],
        ),
        out_shape=jax.ShapeDtypeStruct((n_rows, D), out_dtype),
        compiler_params=_params(("arbitrary",), 32),
        name="row_gather",
    )(idx, live, src)


def _moe(hf, logits, tok, w1, w3, w2, layer):
    n_tok = tok.shape[0]
    n_tiles = (2 * n_tok) // MOE_TM + N_EXP
    top_val, top_idx = lax.top_k(logits, 2)
    top_p = jax.nn.softmax(top_val, axis=-1)
    e_flat = top_idx.reshape(-1)
    onehot = (e_flat[:, None] == jnp.arange(N_EXP)[None, :]).astype(jnp.int32)
    rank = jnp.take_along_axis(jnp.cumsum(onehot, axis=0) - onehot, e_flat[:, None], axis=1)[:, 0]
    count = jnp.sum(onehot, axis=0)
    tiles_per = (count + MOE_TM - 1) // MOE_TM
    tile_end = jnp.cumsum(tiles_per)
    group_start = (tile_end - tiles_per) * MOE_TM
    pos = group_start[e_flat] + rank
    n_valid = tile_end[-1:].astype(jnp.int32)
    tile_expert = jnp.minimum(jnp.searchsorted(tile_end, jnp.arange(n_tiles), side="right"),
                              N_EXP - 1).astype(jnp.int32)
    rows = n_tiles * MOE_TM
    src = jnp.zeros((rows,), jnp.int32).at[pos].set(jnp.repeat(tok, 2))
    scale = jnp.zeros((rows, 1), F32).at[pos, 0].set(top_p.reshape(-1))
    xs = _row_gather(hf, src, 1, BF, n_live_rows=n_valid * MOE_TM)
    ys = _ffn(xs, w1, w3, w2, tile_expert + layer * N_EXP, n_valid, scale, tm=MOE_TM, tf=FF_TILE)
    return _row_gather(ys, pos.reshape(n_tok, 2).T.reshape(-1), 2, F32)


def _low_rank_w_in(w):
    return jnp.concatenate([w[:, 2048:2304], w[:, 3328:3456], jnp.zeros((D, 128), w.dtype)], axis=1)


def _rows_at(mat, off, n=W):
    return jnp.zeros((n, W), F32).at[off:off + mat.shape[0]].set(mat)


def _pad_rows(rows, n):
    z = jnp.stack(rows)
    return jnp.concatenate([z, jnp.zeros((n - z.shape[0],) + z.shape[1:], z.dtype)], axis=0)


def kernel(x, c, ctx, c_ctx, ada_w, ada_b, norm_mix, norm_ffn, w_in, na_rel_bias, rw_mu_kvr, rw_mu_lora, rw_w0, rw_w2, rw_a0, rw_a2, rw_k_k, rw_k_a, rw_r_k, rw_g2, rw_ln_w, rw_ln_b, sc_conv, w_branch, w_out, ffn_w1, ffn_w3, ffn_w2, moe_router, moe_w1, moe_w3, moe_w2, norm_final):
    xs = jnp.concatenate([ctx, x], axis=1)
    cvec = jnp.concatenate([c, c_ctx[None], jnp.zeros((3, D), F32)], axis=0)
    mods = _ada(cvec, ada_w, ada_b).reshape(DEPTH, 8, 6, D)

    def mod_table(gate, scale, shift):
        def pick(sel):
            if sel is None:
                return jnp.zeros((NB, 2, D), F32)
            v = mods[sel[0], :, sel[1]]
            return jnp.stack([jnp.broadcast_to(v[4], (NB, D)), v[:NB]], axis=1)
        t = jnp.stack([pick(gate), pick(scale), pick(shift)], axis=2)
        return jnp.concatenate([t, jnp.zeros((NB, 2, 5, D), F32)], axis=2)

    head_id = np.arange(W) // HD
    ones_bd = jnp.asarray((head_id[:, None] == head_id[None, :]).astype(np.float32), BF)
    chan, cos, nsin, fscale = _dft_mats()
    chan, cos, nsin, fscale = jnp.asarray(chan, BF), jnp.asarray(cos, BF), jnp.asarray(nsin, BF), jnp.asarray(fscale)

    delta = None
    for l in range(DEPTH):
        mod = mod_table(None if l == 0 else (l - 1, 5), (l, 1), (l, 0))
        if l == 0:
            (h,) = _resid_norm(xs, None, mod, norm_mix[l])
        else:
            xs, h = _resid_norm(xs, delta, mod, norm_mix[l])
        u2 = _w_in(h.reshape(M, D), w_in[l])
        u3 = u2.reshape(NB, S, U_COLS)
        ul3 = _matmul(h.reshape(M, D), _low_rank_w_in(w_in[l]), tm=1024, tn=W, out_dtype=BF,
                      name="w_in_low_rank").reshape(NB, S, W)

        y_na = _attention(u3, _na_bias_table(na_rel_bias[l]))

        lora_mu = jnp.zeros((2, W), F32)
        lora_mu = lora_mu.at[:, LORA_W:LORA_W + 2 * LORA].set(
            jnp.broadcast_to(rw_mu_lora[l][:, 0].reshape(1, 2 * LORA), (2, 2 * LORA)))
        lora_mu = lora_mu.at[:, LORA_A:LORA_A + 2 * LORA].set(
            jnp.broadcast_to(rw_mu_lora[l][:, 1].reshape(1, 2 * LORA), (2, 2 * LORA)))
        vec = jnp.stack([_pad_rows([rw_mu_kvr[l][d, 0], rw_mu_kvr[l][d, 1], rw_mu_kvr[l][d, 2], lora_mu[d],
                                    rw_w0[l][d], rw_a0[l][d], rw_k_k[l], rw_k_a[l], rw_r_k[l]], 16)
                         for d in range(2)])
        w2e = jnp.stack([_rows_at(rw_w2[l][d], LORA * d, 2 * LORA) for d in range(2)])
        a2e = jnp.stack([_rows_at(rw_a2[l][d], LORA * d, 2 * LORA) for d in range(2)])
        seqs = [_rw_prep(u3, ul3, vec, w2e, a2e, ones_bd, d) for d in range(2)]
        ys = _rw_scan(seqs[0][:6], seqs[1][:6])
        bons = [seqs[0][6], seqs[1][6]]
        y_rw = _rw_out(ys[0], ys[1], bons[0], bons[1], ul3, _rows_at(rw_g2[l], LORA_G).astype(BF),
                       ones_bd / HD, _pad_rows([rw_ln_w[l], rw_ln_b[l]], 8))

        y_fn = _fourier(u2, chan, cos, nsin, fscale)
        y_sc = _short_conv(u3, _pad_rows([sc_conv[l][0], sc_conv[l][1], sc_conv[l][2]], 8))

        merged = _merge([z.reshape(M, W) for z in (y_na, y_rw, y_fn, y_sc)], u2, w_branch[l].astype(BF))
        merged = merged.reshape(NB, S, D)
        w_o = w_out[l].astype(BF)

        mod = mod_table((l, 2), (l, 4), (l, 3))
        i = l // 2
        last = l == DEPTH - 1
        if l % 2 == 0:
            xs, hf = _resid_norm(xs, merged, mod, norm_ffn[l], proj=w_o)
            delta = _ffn(hf.reshape(M, D), ffn_w1, ffn_w3, ffn_w2,
                         jnp.full((M // 1024,), i, jnp.int32), jnp.full((1,), M // 1024, jnp.int32),
                         jnp.ones((M, 1), F32), tm=1024, tf=FF_TILE).reshape(NB, S, D)
            delta = delta[:, CTX:] if last else delta
        else:
            router = jnp.concatenate([moe_router[i], jnp.zeros((D, 128 - N_EXP), F32)], axis=1)
            xs, hf, logits = _resid_norm(xs, merged, mod, norm_ffn[l], router=router, proj=w_o)
            n_moe = moe_w1.shape[0]
            first = CTX if last else 0
            tok = jnp.arange(M, dtype=jnp.int32).reshape(NB, S)[:, first:].reshape(-1)
            delta = _moe(hf.reshape(M, D), logits[:, first:, :N_EXP].reshape(-1, N_EXP), tok,
                         moe_w1.reshape(n_moe * N_EXP, D, D_FF),
                         moe_w3.reshape(n_moe * N_EXP, D, D_FF), moe_w2.reshape(n_moe * N_EXP, D_FF, D), i)
            delta = delta.reshape(NB, -1, D)

    mod = mod_table((DEPTH - 1, 5), None, None)
    (out,) = _resid_norm(xs, delta, mod, norm_final, final=True)
    return out
```

```python
import functools

import jax
import jax.numpy as jnp
import numpy as np
from jax import lax
from jax.experimental import pallas as pl
from jax.experimental.pallas import tpu as pltpu

BF = jnp.bfloat16
F32 = jnp.float32
HI = lax.Precision.HIGHEST

D = 2048
NB = 4
T = 2048
CTX = 256
S = CTX + T
M = NB * S
DEPTH = 4
GRID_W = 64
ROWS = T // GRID_W
WIN_H = 8
WIN_W = 16
HEADS = 8
HD = 64
W = HEADS * HD
LORA = 64
D_FF = 5632
N_EXP = 8
RMS_EPS = 1e-6
GN_EPS = 64e-5
NEG = -1e30

U_NA_K, U_NA_V, U_RW_K, U_RW_V, U_NA_Q, U_RW_R, U_FN_X, U_SC_B, U_SC_C, U_SC_X, U_GATES = range(11)
W_IN_TILE = 1024
W_IN_SRC = (0, 1024, 2304, 3456, 4480) + tuple(5504 + W_IN_TILE * t for t in range(8))
U_COLS = W_IN_TILE * len(W_IN_SRC)
LORA_W, LORA_A, LORA_G = 0, 128, 256

CHUNK = 64
N_CHUNK = S // CHUNK
CTX_CHUNKS = CTX // CHUNK
TOK_TILE = 256
MOE_TM = 1024
GATHER_ROWS = 256
FF_TILE = 512
HALO = 16


def _params(sem, vmem_mb):
    return pltpu.CompilerParams(dimension_semantics=sem, vmem_limit_bytes=vmem_mb << 20)


def _dot(a, b, **kw):
    return jnp.dot(a, b, preferred_element_type=F32, **kw)


def _dot_nt(a, b):
    return lax.dot_general(a, b, (((1,), (1,)), ((), ())), preferred_element_type=F32)


def _dot_tn(a, b):
    return lax.dot_general(a, b, (((0,), (0,)), ((), ())), preferred_element_type=F32)


def _head_sum(z, sel):
    hi = z.astype(BF)
    lo = (z - hi.astype(F32)).astype(BF)
    return _dot(hi, sel) + _dot(lo, sel)


def _ada_kernel(c_ref, w_ref, b_ref, o_ref):
    c = c_ref[...]
    a = (c * jax.nn.sigmoid(c)).astype(BF)
    o_ref[0] = _dot(a, w_ref[0].astype(BF)) + b_ref[0]


def _ada(cvec, ada_w, ada_b):
    tn = 1024
    return pl.pallas_call(
        _ada_kernel,
        grid=(DEPTH, 6 * D // tn),
        in_specs=[
            pl.BlockSpec((8, D), lambda l, j: (0, 0)),
            pl.BlockSpec((1, D, tn), lambda l, j: (l, 0, j)),
            pl.BlockSpec((1, 1, tn), lambda l, j: (l, 0, j)),
        ],
        out_specs=pl.BlockSpec((1, 8, tn), lambda l, j: (l, 0, j)),
        out_shape=jax.ShapeDtypeStruct((DEPTH, 8, 6 * D), F32),
        compiler_params=_params(("arbitrary", "arbitrary"), 40),
        name="ada",
    )(cvec, ada_w, ada_b.reshape(DEPTH, 1, 6 * D))


def _rn_kernel(*refs, has_delta, has_proj, has_router, final):
    it = iter(refs)
    x_ref = next(it)
    d_ref = next(it) if has_delta else None
    p_ref = next(it) if has_proj else None
    mod_ref = next(it)
    g_ref = next(it)
    r_ref = next(it) if has_router else None
    xo_ref = next(it) if (has_delta and not final) else None
    h_ref = next(it)
    lg_ref = next(it) if has_router else None
    x = x_ref[0]
    mod = mod_ref[0, 0]
    if has_delta:
        d = _dot(d_ref[0], p_ref[...]) if has_proj else d_ref[0]
        x = x + mod[0:1] * d
        if xo_ref is not None:
            xo_ref[0] = x
    y = x * lax.rsqrt(jnp.mean(x * x, axis=-1, keepdims=True) + RMS_EPS) * g_ref[...]
    if final:
        h_ref[0] = y
        return
    h = y * (1.0 + mod[1:2]) + mod[2:3]
    h_ref[0] = h.astype(h_ref.dtype)
    if has_router:
        h_hi = h.astype(BF)
        h_lo = (h - h_hi.astype(F32)).astype(BF)
        lg_ref[0] = _dot(h_hi, r_ref[0]) + (_dot(h_lo, r_ref[0]) + _dot(h_hi, r_ref[1]))


def _resid_norm(x, delta, mod, g, router=None, final=False, proj=None):
    has_delta = delta is not None
    has_router = router is not None
    has_proj = proj is not None
    n_t = S // TOK_TILE
    off = CTX // TOK_TILE if final else 0
    grid = (NB, n_t - off)
    tok = pl.BlockSpec((1, TOK_TILE, D), lambda b, i: (b, i + off, 0))
    in_specs = [tok] + ([pl.BlockSpec((1, TOK_TILE, D), lambda b, i: (b, i, 0))] if has_delta else [])
    in_specs += [pl.BlockSpec((D, D), lambda b, i: (0, 0), pipeline_mode=pl.Buffered(1))] if has_proj else []
    in_specs += [
        pl.BlockSpec((1, 1, 8, D), lambda b, i: (b, jnp.minimum(i + off, 1), 0, 0)),
        pl.BlockSpec((1, D), lambda b, i: (0, 0)),
    ]
    args = [x] + ([delta] if has_delta else []) + ([proj] if has_proj else []) + [mod, g.reshape(1, D)]
    out_specs, out_shape = [], []
    if final:
        out_specs.append(pl.BlockSpec((1, TOK_TILE, D), lambda b, i: (b, i, 0)))
        out_shape.append(jax.ShapeDtypeStruct((NB, T, D), F32))
    else:
        if has_delta:
            out_specs.append(tok)
            out_shape.append(jax.ShapeDtypeStruct((NB, S, D), F32))
        out_specs.append(tok)
        out_shape.append(jax.ShapeDtypeStruct((NB, S, D), F32 if has_router else BF))
    if has_router:
        in_specs.append(pl.BlockSpec((2, D, 128), lambda b, i: (0, 0, 0)))
        r_hi = router.astype(BF)
        args.append(jnp.stack([r_hi, (router - r_hi.astype(F32)).astype(BF)]))
        out_specs.append(pl.BlockSpec((1, TOK_TILE, 128), lambda b, i: (b, i, 0)))
        out_shape.append(jax.ShapeDtypeStruct((NB, S, 128), F32))
    return pl.pallas_call(
        functools.partial(_rn_kernel, has_delta=has_delta, has_proj=has_proj, has_router=has_router, final=final),
        grid=grid,
        in_specs=in_specs,
        out_specs=out_specs,
        out_shape=out_shape,
        compiler_params=_params(("arbitrary", "arbitrary"), 40),
        name="resid_norm",
    )(*args)


def _mm_kernel(a_ref, w_ref, o_ref):
    o_ref[...] = _dot(a_ref[...].astype(BF), w_ref[...].astype(BF)).astype(o_ref.dtype)


def _matmul(a, w, *, tm, tn, out_dtype, a_col_blk=0, k=None, name="matmul"):
    m = a.shape[0]
    k = k or a.shape[1]
    n = w.shape[1]
    return pl.pallas_call(
        _mm_kernel,
        grid=(n // tn, m // tm),
        in_specs=[
            pl.BlockSpec((tm, k), lambda j, i: (i, a_col_blk)),
            pl.BlockSpec((k, tn), lambda j, i: (0, j)),
        ],
        out_specs=pl.BlockSpec((tm, tn), lambda j, i: (i, j)),
        out_shape=jax.ShapeDtypeStruct((m, n), out_dtype),
        compiler_params=_params(("arbitrary", "arbitrary"), 48),
        name=name,
    )(a, w)


def _w_in_kernel(off_ref, a_ref, w_ref, o_ref, wb_ref):
    @pl.when(pl.program_id(1) == 0)
    def _():
        wb_ref[...] = w_ref[0].astype(BF)

    o_ref[...] = _dot(a_ref[...], wb_ref[...]).astype(o_ref.dtype)


def _w_in(h, w, layer):
    tm = 512
    return pl.pallas_call(
        _w_in_kernel,
        grid_spec=pltpu.PrefetchScalarGridSpec(
            num_scalar_prefetch=1,
            grid=(len(W_IN_SRC), M // tm),
            in_specs=[
                pl.BlockSpec((tm, D), lambda j, i, off: (i, 0)),
                pl.BlockSpec((pl.Element(1), pl.Element(D), pl.Element(W_IN_TILE)),
                             lambda j, i, off: (layer, 0, pl.multiple_of(off[j], 128))),
            ],
            out_specs=pl.BlockSpec((tm, W_IN_TILE), lambda j, i, off: (i, j)),
            scratch_shapes=[pltpu.VMEM((D, W_IN_TILE), BF)],
        ),
        out_shape=jax.ShapeDtypeStruct((M, U_COLS), BF),
        compiler_params=_params(("arbitrary", "arbitrary"), 48),
        name="w_in",
    )(jnp.asarray(W_IN_SRC, jnp.int32), h, w)


def _na_window(r):
    rr = r - CTX // GRID_W
    rs = jnp.clip(rr - WIN_H // 2, 0, ROWS - WIN_H)
    return rr, rs


def _na_kernel(q_ref, k_ref, v_ref, bias_ref, o_ref):
    _, rs = _na_window(pl.program_id(1))
    kstart = pl.multiple_of(CTX + GRID_W * rs, GRID_W)
    lo = lax.broadcasted_iota(jnp.int32, (GRID_W, 128), 1) < HD
    n_nb = WIN_H * GRID_W
    pairs = [slice(128 * p, 128 * p + 128) for p in range(HEADS // 2)]
    rowmax = lambda z: jnp.max(z, axis=-1, keepdims=True)
    rowsum = lambda z: jnp.sum(z, axis=-1, keepdims=True)
    q = [q_ref[0, :, cs].astype(F32) * (HD ** -0.5) for cs in pairs]
    q2 = [jnp.concatenate([jnp.where(lo, z, 0.0), jnp.where(lo, 0.0, z)], axis=0).astype(BF) for z in q]
    s_nb = [_dot_nt(z, k_ref[0, pl.ds(kstart, n_nb), cs]) + bias_ref[0, 128 * p:128 * p + 128, :]
            for p, (z, cs) in enumerate(zip(q2, pairs))]
    s_cx = [_dot_nt(z, k_ref[0, 0:CTX, cs]) for z, cs in zip(q2, pairs)]
    m = [jnp.maximum(rowmax(a), rowmax(b)) for a, b in zip(s_nb, s_cx)]
    e_nb = [jnp.exp(a - mi) for a, mi in zip(s_nb, m)]
    e_cx = [jnp.exp(b - mi) for b, mi in zip(s_cx, m)]
    l = [rowsum(a) + rowsum(b) for a, b in zip(e_nb, e_cx)]
    o2 = [(_dot(a.astype(BF), v_ref[0, pl.ds(kstart, n_nb), cs]) + _dot(b.astype(BF), v_ref[0, 0:CTX, cs])) / li
          for a, b, li, cs in zip(e_nb, e_cx, l, pairs)]
    for z, cs in zip(o2, pairs):
        o_ref[0, :, cs] = jnp.where(lo, z[0:GRID_W], z[GRID_W:]).astype(o_ref.dtype)


def _na_bias_table(rel_bias):
    cols = np.arange(GRID_W)
    col_start = np.clip(cols - WIN_W // 2, 0, GRID_W - WIN_W)
    col_ok = (cols[None, :] >= col_start[:, None]) & (cols[None, :] < col_start[:, None] + WIN_W)
    dc = np.clip(cols[None, :] - cols[:, None] + (WIN_W - 1), 0, 2 * WIN_W - 2)
    pick = (dc[None] == np.arange(2 * WIN_W - 1)[:, None, None]).astype(np.float32)
    by_dr = jnp.einsum("hac,cqk->haqk", rel_bias.astype(F32), pick, precision=HI)
    by_dr = jnp.where(col_ok[None, None], by_dr, NEG)
    tab = jnp.stack([by_dr[:, d0:d0 + WIN_H] for d0 in range(WIN_H)])
    tab = jnp.transpose(tab, (0, 1, 3, 2, 4)).reshape(WIN_H, HEADS * GRID_W, WIN_H * GRID_W)
    return jnp.concatenate([tab, jnp.full((1,) + tab.shape[1:], NEG, F32)], axis=0)


def _attention(u3, bias_tab):
    def bias_idx(b, r):
        rr, rs = _na_window(r)
        return (jnp.where(rr < 0, WIN_H, rs - rr + WIN_H - 1), 0, 0)

    return pl.pallas_call(
        _na_kernel,
        grid=(NB, S // GRID_W),
        in_specs=[
            pl.BlockSpec((1, GRID_W, W), lambda b, r: (b, r, U_NA_Q)),
            pl.BlockSpec((1, S, W), lambda b, r: (b, 0, U_NA_K)),
            pl.BlockSpec((1, S, W), lambda b, r: (b, 0, U_NA_V)),
            pl.BlockSpec((1, HEADS * GRID_W, WIN_H * GRID_W), bias_idx),
        ],
        out_specs=pl.BlockSpec((1, GRID_W, W), lambda b, r: (b, r, 0)),
        out_shape=jax.ShapeDtypeStruct((NB, S, W), BF),
        compiler_params=_params(("arbitrary", "arbitrary"), 48),
        name="attention",
    )(u3, u3, u3, bias_tab)


def _shift_prev(z, halo_row, first_is_start, rev):
    n = z.shape[0]
    row = lax.broadcasted_iota(jnp.int32, z.shape, 0)
    edge = jnp.where(first_is_start, 0.0, halo_row)
    if rev:
        return jnp.where(row == n - 1, edge, pltpu.roll(z, n - 1, 0))
    return jnp.where(row == 0, edge, pltpu.roll(z, 1, 0))


def _rw_prep_kernel(k_ref, v_ref, r_ref, l_ref, kh_ref, vh_ref, rh_ref, lh_ref, vec_ref, w2_ref, a2_ref,
                    ones_ref, lw_o, k_o, v_o, kk_o, b_o, r_o, bonus_o, *, rev):
    i = pl.program_id(1)
    n_t = S // TOK_TILE
    start = (i == n_t - 1) | (i == 0) if rev else (i <= 1)
    hrow = 0 if rev else HALO - 1
    vec = vec_ref[0]
    mu_k, mu_v, mu_r, mu_l, w0, a0, k_k, k_a, r_k = (vec[j:j + 1] for j in range(9))

    def lerp(ref, href, mu):
        z = ref[0].astype(F32)
        prev = _shift_prev(z, href[0].astype(F32)[hrow:hrow + 1, :], start, rev)
        return z + (prev - z) * mu

    lz = lerp(l_ref, lh_ref, mu_l)
    wz = w0 + _dot(jnp.tanh(lz[:, LORA_W:LORA_W + 128]), w2_ref[0], precision=HI)
    a = jax.nn.sigmoid(a0 + _dot(lz[:, LORA_A:LORA_A + 128], a2_ref[0], precision=HI))
    softplus = jnp.maximum(-wz, 0.0) + jnp.log1p(jnp.exp(-jnp.abs(wz)))
    lw_o[0] = -jnp.exp(-softplus - 0.5)
    k1 = lerp(k_ref, kh_ref, mu_k)
    kkr = k1 * k_k
    ones = ones_ref[...]
    nrm = jnp.sqrt(_head_sum(kkr * kkr, ones))
    kk = kkr / jnp.maximum(nrm, 1e-12)
    k2 = k1 * (1.0 + (a - 1.0) * k_a)
    v1 = lerp(v_ref, vh_ref, mu_v)
    r1 = lerp(r_ref, rh_ref, mu_r)
    k_o[0] = k2
    v_o[0] = v1
    kk_o[0] = kk
    b_o[0] = kk * a
    r_o[0] = r1
    bonus_o[0] = _head_sum(r1 * k2 * r_k, ones) * v1


def _rw_prep(u3, ul3, vec, w2e, a2e, ones_bd, d):
    rev = d == 1
    n_t = S // TOK_TILE
    hb = TOK_TILE // HALO

    def tile(blk):
        return pl.BlockSpec((1, TOK_TILE, W), lambda b, i: (b, i, blk))

    def halo(blk):
        if rev:
            return pl.BlockSpec((1, HALO, W), lambda b, i: (b, jnp.minimum((i + 1) * hb, S // HALO - 1), blk))
        return pl.BlockSpec((1, HALO, W), lambda b, i: (b, jnp.maximum(i * hb - 1, 0), blk))

    blks = (U_RW_K, U_RW_V, U_RW_R, 0)
    out = pl.BlockSpec((1, TOK_TILE, W), lambda b, i: (b, i, 0))
    return pl.pallas_call(
        functools.partial(_rw_prep_kernel, rev=rev),
        grid=(NB, n_t),
        in_specs=[tile(x) for x in blks] + [halo(x) for x in blks] + [
            pl.BlockSpec((1, 16, W), lambda b, i: (d, 0, 0)),
            pl.BlockSpec((1, 2 * LORA, W), lambda b, i: (d, 0, 0)),
            pl.BlockSpec((1, 2 * LORA, W), lambda b, i: (d, 0, 0)),
            pl.BlockSpec((W, W), lambda b, i: (0, 0)),
        ],
        out_specs=[out] * 7,
        out_shape=[jax.ShapeDtypeStruct((NB, S, W), F32)] * 7,
        compiler_params=_params(("arbitrary", "arbitrary"), 48),
        name="rw_prep",
    )(u3, u3, u3, ul3, u3, u3, u3, ul3, vec, w2e, a2e, ones_bd)


def _pair_rows(z, lo):
    return jnp.concatenate([jnp.where(lo, z, 0.0), jnp.where(lo, 0.0, z)], axis=0)


def _rw_scan_kernel(*refs):
    ins, y_refs, s_ref = (refs[0:6], refs[6:12]), refs[12:14], refs[14]

    @pl.when(pl.program_id(1) == 0)
    def _():
        s_ref[...] = jnp.zeros_like(s_ref)

    n = CHUNK
    t_i = lax.broadcasted_iota(jnp.int32, (n, n), 0)
    s_i = lax.broadcasted_iota(jnp.int32, (n, n), 1)
    t2 = lax.broadcasted_iota(jnp.int32, (2 * n, 2 * n), 0)
    s2 = lax.broadcasted_iota(jnp.int32, (2 * n, 2 * n), 1)
    same = (t2 >= n) == (s2 >= n)
    tt, ss = t2 & (n - 1), s2 & (n - 1)
    eye2 = t2 == s2
    lo = lax.broadcasted_iota(jnp.int32, (n, 128), 1) < HD
    incl1 = (s_i <= t_i, s_i >= t_i)
    incl2 = (same & (ss <= tt), same & (ss >= tt))
    strict2 = (same & (ss < tt), same & (ss > tt))
    last = (n - 1, 0)
    bf = lambda z: z.astype(BF)

    lw_all = [ins[d][0][0] for d in range(2)]
    cum_all = [_dot(incl1[d].astype(F32), lw_all[d], precision=HI) for d in range(2)]
    chains = [(d, p) for d in range(2) for p in range(HEADS // 2)]

    def operands(d, p):
        cs = slice(128 * p, 128 * p + 128)
        _, k_ref, v_ref, kk_ref, b_ref, r_ref = ins[d]
        lw, cum = lw_all[d][:, cs], cum_all[d][:, cs]
        tot = cum[last[d]:last[d] + 1, :]
        g, gp, gi, gc = jnp.exp(cum), jnp.exp(cum - lw), jnp.exp(-cum), jnp.exp(tot - cum)
        kk, b, k, r, v = kk_ref[0, :, cs], b_ref[0, :, cs], k_ref[0, :, cs], r_ref[0, :, cs], v_ref[0, :, cs]
        return dict(
            a2=bf(_pair_rows(-kk * gp, lo)), r2=_pair_rows(r * g, lo), b2=bf(_pair_rows(b * gi, lo)),
            k2=bf(_pair_rows(k * gi, lo)), bt2=bf(_pair_rows(b * gc, lo)), kt2=bf(_pair_rows(k * gc, lo)),
            v2=bf(_pair_rows(v, lo)), gtot=jnp.exp(tot))

    op = [operands(d, p) for d, p in chains]
    sc = [_dot_nt(jnp.concatenate([o["a2"], bf(o["r2"])], axis=0), jnp.concatenate([o["b2"], o["k2"]], axis=0))
          for o in op]
    m_ab = [jnp.where(strict2[d], s[:2 * n, :2 * n], 0.0) for (d, _), s in zip(chains, sc)]
    m_ak = [bf(jnp.where(strict2[d], s[:2 * n, 2 * n:], 0.0)) for (d, _), s in zip(chains, sc)]
    m_rb = [bf(jnp.where(incl2[d], s[2 * n:, :2 * n], 0.0)) for (d, _), s in zip(chains, sc)]
    m_rk = [bf(jnp.where(incl2[d], s[2 * n:, 2 * n:], 0.0)) for (d, _), s in zip(chains, sc)]
    x = [jnp.where(eye2, 1.0, 0.0) + m for m in m_ab]
    pw = m_ab
    for _ in range(int(np.log2(n)) - 1):
        pw = [_dot(bf(z), bf(z)) for z in pw]
        x = [xi + _dot(bf(xi), bf(z)) for xi, z in zip(x, pw)]
    xb = [bf(z) for z in x]
    mv = [bf(_dot(m, o["v2"])) for m, o in zip(m_ak, op)]
    wu = [bf(_dot(z, jnp.concatenate([o["a2"], m], axis=1))) for z, o, m in zip(xb, op, mv)]
    rw = [_dot(m, z) for m, z in zip(m_rb, wu)]
    bw = [_dot_tn(o["bt2"], z) for o, z in zip(op, wu)]
    rm = [o["r2"] + z[:, :128] for o, z in zip(op, rw)]
    yu = [z[:, 128:] + _dot(mk, o["v2"]) for z, mk, o in zip(rw, m_rk, op)]
    tm = [jnp.where(eye2, o["gtot"], 0.0) + z[:, :128] for o, z in zip(op, bw)]
    sv = [z[:, 128:] + _dot_tn(o["kt2"], o["v2"]) for o, z in zip(op, bw)]
    st = [bf(s_ref[d, p]) for d, p in chains]
    y2 = [_dot(bf(r), s) + y for r, s, y in zip(rm, st, yu)]
    s_new = [_dot(bf(t), s) + v for t, s, v in zip(tm, st, sv)]
    for (d, p), y, s in zip(chains, y2, s_new):
        y_refs[d][0, :, 128 * p:128 * p + 128] = y[:n] + y[n:]
        s_ref[d, p] = s


def _rw_scan(fwd, rev):
    def rev_idx(b, c):
        return (b, jnp.where(c < CTX_CHUNKS, CTX_CHUNKS - 1 - c, N_CHUNK + CTX_CHUNKS - 1 - c), 0)

    f_spec = pl.BlockSpec((1, CHUNK, W), lambda b, c: (b, c, 0))
    r_spec = pl.BlockSpec((1, CHUNK, W), rev_idx)
    return pl.pallas_call(
        _rw_scan_kernel,
        grid=(NB, N_CHUNK),
        in_specs=[f_spec] * 6 + [r_spec] * 6,
        out_specs=[f_spec, r_spec],
        out_shape=[jax.ShapeDtypeStruct((NB, S, W), F32)] * 2,
        scratch_shapes=[pltpu.VMEM((2, HEADS // 2, 128, 128), F32)],
        compiler_params=_params(("arbitrary", "arbitrary"), 32),
        name="rw_scan",
    )(*fwd, *rev)


def _rw_out_kernel(y0_ref, y1_ref, b0_ref, b1_ref, l_ref, g2_ref, avg_ref, ln_ref, o_ref):
    y = y0_ref[0] + y1_ref[0]
    avg = avg_ref[...]
    mu = _head_sum(y, avg)
    yc = y - mu
    var = _head_sum(yc * yc, avg)
    ln = ln_ref[...]
    y = yc * lax.rsqrt(var + GN_EPS) * ln[0:1] + ln[1:2] + b0_ref[0] + b1_ref[0]
    gate = _dot(jax.nn.sigmoid(l_ref[0].astype(F32)).astype(BF), g2_ref[...])
    o_ref[0] = (y * gate).astype(o_ref.dtype)


def _rw_out(y0, y1, bon0, bon1, ul3, g2e, avg_bd, ln):
    tile = pl.BlockSpec((1, TOK_TILE, W), lambda b, i: (b, i, 0))
    return pl.pallas_call(
        _rw_out_kernel,
        grid=(NB, S // TOK_TILE),
        in_specs=[tile] * 4 + [
            tile,
            pl.BlockSpec((W, W), lambda b, i: (0, 0)),
            pl.BlockSpec((W, W), lambda b, i: (0, 0)),
            pl.BlockSpec((8, W), lambda b, i: (0, 0)),
        ],
        out_specs=tile,
        out_shape=jax.ShapeDtypeStruct((NB, S, W), BF),
        compiler_params=_params(("arbitrary", "arbitrary"), 32),
        name="rw_out",
    )(y0, y1, bon0, bon1, ul3, g2e, avg_bd, ln)


def _dft_mats():
    gd = W // 4
    cc = np.arange(gd)
    ang = 2 * np.pi * ((cc[:, None] * cc[None, :]) % gd) / gd
    chan = np.zeros((W, 2 * W), np.float32)
    for g in range(4):
        chan[g * gd:(g + 1) * gd, g * gd:(g + 1) * gd] = np.cos(ang)
        chan[g * gd:(g + 1) * gd, W + g * gd:W + (g + 1) * gd] = np.sin(ang)
    cos = np.zeros((S, S), np.float32)
    nsin = np.zeros((S, S), np.float32)
    scale = np.zeros((S, 1), np.float32)
    for lo, n in ((0, CTX), (CTX, T)):
        tt = np.arange(n)
        a = 2 * np.pi * ((tt[:, None] * tt[None, :]) % n) / n
        cos[lo:lo + n, lo:lo + n] = np.cos(a)
        nsin[lo:lo + n, lo:lo + n] = -np.sin(a)
        scale[lo:lo + n] = (n * gd) ** -0.5
    return chan, cos, nsin, scale


def _fn_tok_kernel(cos_ref, nsin_ref, z_ref, sc_ref, o_ref):
    z = z_ref[0]
    f = _dot(cos_ref[...], z[:, 0:W]) + _dot(nsin_ref[...], z[:, W:2 * W])
    o_ref[0] = (f * sc_ref[...]).astype(o_ref.dtype)


def _fourier(u2, chan, cos, nsin, scale):
    zcs = _matmul(u2, chan, tm=1024, tn=2 * W, out_dtype=BF, a_col_blk=U_FN_X, k=W, name="fn_chan")
    return pl.pallas_call(
        _fn_tok_kernel,
        grid=(NB, S // TOK_TILE),
        in_specs=[
            pl.BlockSpec((TOK_TILE, S), lambda b, i: (i, 0)),
            pl.BlockSpec((TOK_TILE, S), lambda b, i: (i, 0)),
            pl.BlockSpec((1, S, 2 * W), lambda b, i: (b, 0, 0)),
            pl.BlockSpec((TOK_TILE, 1), lambda b, i: (i, 0)),
        ],
        out_specs=pl.BlockSpec((1, TOK_TILE, W), lambda b, i: (b, i, 0)),
        out_shape=jax.ShapeDtypeStruct((NB, S, W), BF),
        compiler_params=_params(("arbitrary", "arbitrary"), 40),
        name="fn_tok",
    )(cos, nsin, zcs.reshape(NB, S, 2 * W), scale)


def _sc_kernel(b_ref, c_ref, x_ref, cp_ref, xp_ref, cn_ref, xn_ref, w_ref, o_ref):
    i = pl.program_id(1)
    n_t = S // TOK_TILE
    f32 = lambda ref: ref[0].astype(F32)
    z = f32(c_ref) * f32(x_ref)
    zp = _shift_prev(z, (f32(cp_ref) * f32(xp_ref))[HALO - 1:HALO, :], i <= 1, False)
    zn = _shift_prev(z, (f32(cn_ref) * f32(xn_ref))[0:1, :], (i == 0) | (i == n_t - 1), True)
    w = w_ref[...]
    o_ref[0] = (f32(b_ref) * (zp * w[0:1] + z * w[1:2] + zn * w[2:3])).astype(o_ref.dtype)


def _short_conv(u3, conv_w):
    hb = TOK_TILE // HALO

    def tile(blk):
        return pl.BlockSpec((1, TOK_TILE, W), lambda b, i: (b, i, blk))

    def prev(blk):
        return pl.BlockSpec((1, HALO, W), lambda b, i: (b, jnp.maximum(i * hb - 1, 0), blk))

    def nxt(blk):
        return pl.BlockSpec((1, HALO, W), lambda b, i: (b, jnp.minimum((i + 1) * hb, S // HALO - 1), blk))

    return pl.pallas_call(
        _sc_kernel,
        grid=(NB, S // TOK_TILE),
        in_specs=[tile(U_SC_B), tile(U_SC_C), tile(U_SC_X), prev(U_SC_C), prev(U_SC_X), nxt(U_SC_C), nxt(U_SC_X),
                  pl.BlockSpec((8, W), lambda b, i: (0, 0))],
        out_specs=pl.BlockSpec((1, TOK_TILE, W), lambda b, i: (b, i, 0)),
        out_shape=jax.ShapeDtypeStruct((NB, S, W), BF),
        compiler_params=_params(("arbitrary", "arbitrary"), 32),
        name="short_conv",
    )(*([u3] * 7), conv_w)


def _merge_kernel(y0_ref, y1_ref, y2_ref, y3_ref, g0_ref, g1_ref, g2_ref, g3_ref, w_ref, o_ref):
    acc = None
    for n, (y_ref, g_ref) in enumerate(((y0_ref, g0_ref), (y1_ref, g1_ref), (y2_ref, g2_ref), (y3_ref, g3_ref))):
        t = jax.nn.sigmoid(g_ref[...].astype(F32)) * _dot(y_ref[...], w_ref[n])
        acc = t if acc is None else acc + t
    o_ref[...] = acc.astype(o_ref.dtype)


def _merge(branches, u2, w_branch):
    tm, tn = 512, 512

    def gate(n):
        return pl.BlockSpec((tm, tn), lambda i, j: (i, U_GATES + n * (D // tn) + j))

    return pl.pallas_call(
        _merge_kernel,
        grid=(M // tm, D // tn),
        in_specs=[pl.BlockSpec((tm, W), lambda i, j: (i, 0))] * 4 + [gate(n) for n in range(4)] + [
            pl.BlockSpec((4, W, tn), lambda i, j: (0, 0, j))],
        out_specs=pl.BlockSpec((tm, tn), lambda i, j: (i, j)),
        out_shape=jax.ShapeDtypeStruct((M, D), BF),
        compiler_params=_params(("arbitrary", "arbitrary"), 32),
        name="merge",
    )(*branches, u2, u2, u2, u2, w_branch)


def _ffn_kernel(te_ref, nv_ref, x_ref, w1_ref, w3_ref, w2_ref, o_ref):
    i, f = pl.program_id(0), pl.program_id(1)

    @pl.when(f == 0)
    def _():
        o_ref[...] = jnp.zeros_like(o_ref)

    @pl.when(i < nv_ref[0])
    def _():
        x = x_ref[...]
        a = _dot(x, w1_ref[0].astype(BF))
        b = _dot(x, w3_ref[0].astype(BF))
        g = (a * jax.nn.sigmoid(a) * b).astype(BF)
        o_ref[...] += _dot(g, w2_ref[0].astype(BF))


def _ffn(x, w1, w3, w2, tile_expert, n_valid, *, tm, tf):
    rows = x.shape[0]
    n_tiles = rows // tm

    n_f = D_FF // tf

    def tile(i, nv):
        return jnp.minimum(i, nv[0] - 1)

    def ff(i, f, nv):
        return jnp.where(i < nv[0], f, n_f - 1)

    once = dict(pipeline_mode=pl.Buffered(1))
    return pl.pallas_call(
        _ffn_kernel,
        grid_spec=pltpu.PrefetchScalarGridSpec(
            num_scalar_prefetch=2,
            grid=(n_tiles, n_f),
            in_specs=[
                pl.BlockSpec((tm, D), lambda i, f, te, nv: (tile(i, nv), 0), **once),
                pl.BlockSpec((1, D, tf), lambda i, f, te, nv: (te[tile(i, nv)], 0, ff(i, f, nv))),
                pl.BlockSpec((1, D, tf), lambda i, f, te, nv: (te[tile(i, nv)], 0, ff(i, f, nv))),
                pl.BlockSpec((1, tf, D), lambda i, f, te, nv: (te[tile(i, nv)], ff(i, f, nv), 0)),
            ],
            out_specs=pl.BlockSpec((tm, D), lambda i, f, te, nv: (i, 0), **once),
        ),
        out_shape=jax.ShapeDtypeStruct((rows, D), F32),
        compiler_params=_params(("arbitrary", "arbitrary"), 56),
        name="ffn",
    )(tile_expert, n_valid, x, w1, w3, w2)


def _gather_kernel(idx_ref, live_ref, src_ref, *rest, n_slots, n_rows, weighted):
    wt_ref = rest[0] if weighted else None
    o_ref, buf, sem = rest[-3:]
    g, n_live = pl.program_id(0), live_ref[0]

    def issue(step, slot):
        def body(k, carry):
            for s in range(n_slots):
                row = idx_ref[s * n_rows + step * GATHER_ROWS + k]
                pltpu.make_async_copy(src_ref.at[pl.ds(row, 1)], buf.at[slot, pl.ds(s * GATHER_ROWS + k, 1)],
                                      sem.at[slot]).start()
            return carry
        lax.fori_loop(0, GATHER_ROWS, body, 0, unroll=8)

    @pl.when(g == 0)
    def _():
        issue(0, 0)

    @pl.when(g + 1 < n_live)
    def _():
        issue(g + 1, (g + 1) % 2)

    @pl.when(g < n_live)
    def _():
        slot = g % 2
        pltpu.make_async_copy(src_ref.at[pl.ds(0, n_slots * GATHER_ROWS)], buf.at[slot], sem.at[slot]).wait()
        acc = None
        for s in range(n_slots):
            rows = buf[slot, s * GATHER_ROWS:(s + 1) * GATHER_ROWS]
            if wt_ref is not None:
                rows = rows * wt_ref[:, s:s + 1]
            acc = rows if acc is None else acc + rows
        o_ref[...] = acc.astype(o_ref.dtype)

    @pl.when(g >= n_live)
    def _():
        o_ref[...] = jnp.zeros_like(o_ref)


def _row_gather(src, idx, n_slots, out_dtype, n_live_rows=None, weights=None):
    n_rows = idx.shape[0] // n_slots
    n_steps = n_rows // GATHER_ROWS
    live = jnp.full((1,), n_steps, jnp.int32) if n_live_rows is None else n_live_rows // GATHER_ROWS
    weighted = weights is not None
    wt_spec = [pl.BlockSpec((GATHER_ROWS, n_slots), lambda g, idx, live: (g, 0))] if weighted else []
    return pl.pallas_call(
        functools.partial(_gather_kernel, n_slots=n_slots, n_rows=n_rows, weighted=weighted),
        grid_spec=pltpu.PrefetchScalarGridSpec(
            num_scalar_prefetch=2,
            grid=(n_steps,),
            in_specs=[pl.BlockSpec(memory_space=pl.ANY)] + wt_spec,
            out_specs=pl.BlockSpec((GATHER_ROWS, D), lambda g, idx, live: (g, 0)),
            scratch_shapes=[pltpu.VMEM((2, n_slots * GATHER_ROWS, D), F32), pltpu.SemaphoreType.DMA((2,))],
        ),
        out_shape=jax.ShapeDtypeStruct((n_rows, D), out_dtype),
        compiler_params=_params(("arbitrary",), 32),
        name="row_gather",
    )(idx, live, src, *([weights] if weighted else []))


def _moe(hf, logits, tok, w1, w3, w2, layer):
    n_tok = tok.shape[0]
    n_tiles = (2 * n_tok) // MOE_TM + N_EXP
    top_val, top_idx = lax.top_k(logits, 2)
    top_p = jax.nn.softmax(top_val, axis=-1)
    e_flat = top_idx.reshape(-1)
    onehot = (e_flat[:, None] == jnp.arange(N_EXP)[None, :]).astype(jnp.int32)
    rank = jnp.take_along_axis(jnp.cumsum(onehot, axis=0) - onehot, e_flat[:, None], axis=1)[:, 0]
    count = jnp.sum(onehot, axis=0)
    tiles_per = (count + MOE_TM - 1) // MOE_TM
    tile_end = jnp.cumsum(tiles_per)
    group_start = (tile_end - tiles_per) * MOE_TM
    pos = group_start[e_flat] + rank
    n_valid = tile_end[-1:].astype(jnp.int32)
    tile_expert = jnp.minimum(jnp.searchsorted(tile_end, jnp.arange(n_tiles), side="right"),
                              N_EXP - 1).astype(jnp.int32)
    rows = n_tiles * MOE_TM
    src = jnp.zeros((rows,), jnp.int32).at[pos].set(jnp.repeat(tok, 2))
    xs = _row_gather(hf, src, 1, BF, n_live_rows=n_valid * MOE_TM)
    ys = _ffn(xs, w1, w3, w2, tile_expert + layer * N_EXP, n_valid, tm=MOE_TM, tf=FF_TILE)
    return _row_gather(ys, pos.reshape(n_tok, 2).T.reshape(-1), 2, F32, weights=top_p)


def _low_rank_w_in(w):
    return jnp.concatenate([w[:, 2048:2304], w[:, 3328:3456], jnp.zeros((D, 128), w.dtype)], axis=1)


def _rows_at(mat, off, n=W):
    return jnp.zeros((n, W), F32).at[off:off + mat.shape[0]].set(mat)


def _pad_rows(rows, n):
    z = jnp.stack(rows)
    return jnp.concatenate([z, jnp.zeros((n - z.shape[0],) + z.shape[1:], z.dtype)], axis=0)


def kernel(x, c, ctx, c_ctx, ada_w, ada_b, norm_mix, norm_ffn, w_in, na_rel_bias, rw_mu_kvr, rw_mu_lora, rw_w0, rw_w2, rw_a0, rw_a2, rw_k_k, rw_k_a, rw_r_k, rw_g2, rw_ln_w, rw_ln_b, sc_conv, w_branch, w_out, ffn_w1, ffn_w3, ffn_w2, moe_router, moe_w1, moe_w3, moe_w2, norm_final):
    xs = jnp.concatenate([ctx, x], axis=1)
    cvec = jnp.concatenate([c, c_ctx[None], jnp.zeros((3, D), F32)], axis=0)
    mods = _ada(cvec, ada_w, ada_b).reshape(DEPTH, 8, 6, D)

    def mod_table(gate, scale, shift):
        def pick(sel):
            if sel is None:
                return jnp.zeros((NB, 2, D), F32)
            v = mods[sel[0], :, sel[1]]
            return jnp.stack([jnp.broadcast_to(v[4], (NB, D)), v[:NB]], axis=1)
        t = jnp.stack([pick(gate), pick(scale), pick(shift)], axis=2)
        return jnp.concatenate([t, jnp.zeros((NB, 2, 5, D), F32)], axis=2)

    head_id = np.arange(W) // HD
    ones_bd = jnp.asarray((head_id[:, None] == head_id[None, :]).astype(np.float32), BF)
    chan, cos, nsin, fscale = _dft_mats()
    chan, cos, nsin, fscale = jnp.asarray(chan, BF), jnp.asarray(cos, BF), jnp.asarray(nsin, BF), jnp.asarray(fscale)

    delta = None
    for l in range(DEPTH):
        mod = mod_table(None if l == 0 else (l - 1, 5), (l, 1), (l, 0))
        if l == 0:
            (h,) = _resid_norm(xs, None, mod, norm_mix[l])
        else:
            xs, h = _resid_norm(xs, delta, mod, norm_mix[l])
        u2 = _w_in(h.reshape(M, D), w_in, l)
        u3 = u2.reshape(NB, S, U_COLS)
        ul3 = _matmul(h.reshape(M, D), _low_rank_w_in(w_in[l]), tm=1024, tn=W, out_dtype=BF,
                      name="w_in_low_rank").reshape(NB, S, W)

        y_na = _attention(u3, _na_bias_table(na_rel_bias[l]))

        lora_mu = jnp.zeros((2, W), F32)
        lora_mu = lora_mu.at[:, LORA_W:LORA_W + 2 * LORA].set(
            jnp.broadcast_to(rw_mu_lora[l][:, 0].reshape(1, 2 * LORA), (2, 2 * LORA)))
        lora_mu = lora_mu.at[:, LORA_A:LORA_A + 2 * LORA].set(
            jnp.broadcast_to(rw_mu_lora[l][:, 1].reshape(1, 2 * LORA), (2, 2 * LORA)))
        vec = jnp.stack([_pad_rows([rw_mu_kvr[l][d, 0], rw_mu_kvr[l][d, 1], rw_mu_kvr[l][d, 2], lora_mu[d],
                                    rw_w0[l][d], rw_a0[l][d], rw_k_k[l], rw_k_a[l], rw_r_k[l]], 16)
                         for d in range(2)])
        w2e = jnp.stack([_rows_at(rw_w2[l][d], LORA * d, 2 * LORA) for d in range(2)])
        a2e = jnp.stack([_rows_at(rw_a2[l][d], LORA * d, 2 * LORA) for d in range(2)])
        seqs = [_rw_prep(u3, ul3, vec, w2e, a2e, ones_bd, d) for d in range(2)]
        ys = _rw_scan(seqs[0][:6], seqs[1][:6])
        bons = [seqs[0][6], seqs[1][6]]
        y_rw = _rw_out(ys[0], ys[1], bons[0], bons[1], ul3, _rows_at(rw_g2[l], LORA_G).astype(BF),
                       ones_bd / HD, _pad_rows([rw_ln_w[l], rw_ln_b[l]], 8))

        y_fn = _fourier(u2, chan, cos, nsin, fscale)
        y_sc = _short_conv(u3, _pad_rows([sc_conv[l][0], sc_conv[l][1], sc_conv[l][2]], 8))

        merged = _merge([z.reshape(M, W) for z in (y_na, y_rw, y_fn, y_sc)], u2, w_branch[l].astype(BF))
        merged = merged.reshape(NB, S, D)
        w_o = w_out[l].astype(BF)

        mod = mod_table((l, 2), (l, 4), (l, 3))
        i = l // 2
        last = l == DEPTH - 1
        if l % 2 == 0:
            xs, hf = _resid_norm(xs, merged, mod, norm_ffn[l], proj=w_o)
            delta = _ffn(hf.reshape(M, D), ffn_w1, ffn_w3, ffn_w2,
                         jnp.full((M // 1024,), i, jnp.int32), jnp.full((1,), M // 1024, jnp.int32),
                         tm=1024, tf=FF_TILE).reshape(NB, S, D)
            delta = delta[:, CTX:] if last else delta
        else:
            router = jnp.concatenate([moe_router[i], jnp.zeros((D, 128 - N_EXP), F32)], axis=1)
            xs, hf, logits = _resid_norm(xs, merged, mod, norm_ffn[l], router=router, proj=w_o)
            n_moe = moe_w1.shape[0]
            first = CTX if last else 0
            tok = jnp.arange(M, dtype=jnp.int32).reshape(NB, S)[:, first:].reshape(-1)
            delta = _moe(hf.reshape(M, D), logits[:, first:, :N_EXP].reshape(-1, N_EXP), tok,
                         moe_w1.reshape(n_moe * N_EXP, D, D_FF),
                         moe_w3.reshape(n_moe * N_EXP, D, D_FF), moe_w2.reshape(n_moe * N_EXP, D_FF, D), i)
            delta = delta.reshape(NB, -1, D)

    mod = mod_table((DEPTH - 1, 5), None, None)
    (out,) = _resid_norm(xs, delta, mod, norm_final, final=True)
    return out
```

```python
import functools

import jax
import jax.numpy as jnp
import numpy as np
from jax import lax
from jax.experimental import pallas as pl
from jax.experimental.pallas import tpu as pltpu

BF = jnp.bfloat16
F32 = jnp.float32
HI = lax.Precision.HIGHEST

D = 2048
NB = 4
T = 2048
CTX = 256
S = CTX + T
M = NB * S
DEPTH = 4
GRID_W = 64
ROWS = T // GRID_W
WIN_H = 8
WIN_W = 16
HEADS = 8
HD = 64
W = HEADS * HD
LORA = 64
D_FF = 5632
N_EXP = 8
RMS_EPS = 1e-6
GN_EPS = 64e-5
NEG = -1e30

U_NA_K, U_NA_V, U_RW_K, U_RW_V, U_NA_Q, U_RW_R, U_FN_X, U_SC_B, U_SC_C, U_SC_X, U_GATES = range(11)
W_IN_TILE = 1024
W_IN_SRC = (0, 1024, 2304, 3456, 4480) + tuple(5504 + W_IN_TILE * t for t in range(8))
U_COLS = W_IN_TILE * len(W_IN_SRC)
LORA_W, LORA_A, LORA_G = 0, 128, 256

CHUNK = 64
N_CHUNK = S // CHUNK
CTX_CHUNKS = CTX // CHUNK
TOK_TILE = 256
MOE_TM = 1024
GATHER_ROWS = 512
FF_TILE = 512
HALO = 16


def _params(sem, vmem_mb):
    return pltpu.CompilerParams(dimension_semantics=sem, vmem_limit_bytes=vmem_mb << 20)


def _dot(a, b, **kw):
    return jnp.dot(a, b, preferred_element_type=F32, **kw)


def _dot_nt(a, b):
    return lax.dot_general(a, b, (((1,), (1,)), ((), ())), preferred_element_type=F32)


def _dot_tn(a, b):
    return lax.dot_general(a, b, (((0,), (0,)), ((), ())), preferred_element_type=F32)


def _head_sum(z, sel):
    hi = z.astype(BF)
    lo = (z - hi.astype(F32)).astype(BF)
    return _dot(hi, sel) + _dot(lo, sel)


def _ada_kernel(c_ref, w_ref, b_ref, o_ref):
    c = c_ref[...]
    a = (c * jax.nn.sigmoid(c)).astype(BF)
    o_ref[0] = _dot(a, w_ref[0].astype(BF)) + b_ref[0]


def _ada(cvec, ada_w, ada_b):
    tn = 1024
    return pl.pallas_call(
        _ada_kernel,
        grid=(DEPTH, 6 * D // tn),
        in_specs=[
            pl.BlockSpec((8, D), lambda l, j: (0, 0)),
            pl.BlockSpec((1, D, tn), lambda l, j: (l, 0, j)),
            pl.BlockSpec((1, 1, tn), lambda l, j: (l, 0, j)),
        ],
        out_specs=pl.BlockSpec((1, 8, tn), lambda l, j: (l, 0, j)),
        out_shape=jax.ShapeDtypeStruct((DEPTH, 8, 6 * D), F32),
        compiler_params=_params(("arbitrary", "arbitrary"), 40),
        name="ada",
    )(cvec, ada_w, ada_b.reshape(DEPTH, 1, 6 * D))


def _rn_kernel(*refs, has_delta, has_proj, has_router, final):
    it = iter(refs)
    x_ref = next(it)
    d_ref = next(it) if has_delta else None
    p_ref = next(it) if has_proj else None
    mod_ref = next(it)
    g_ref = next(it)
    r_ref = next(it) if has_router else None
    xo_ref = next(it) if (has_delta and not final) else None
    h_ref = next(it)
    lg_ref = next(it) if has_router else None
    x = x_ref[0]
    mod = mod_ref[0, 0]
    if has_delta:
        d = _dot(d_ref[0], p_ref[...]) if has_proj else d_ref[0]
        x = x + mod[0:1] * d
        if xo_ref is not None:
            xo_ref[0] = x
    y = x * lax.rsqrt(jnp.mean(x * x, axis=-1, keepdims=True) + RMS_EPS) * g_ref[...]
    if final:
        h_ref[0] = y
        return
    h = y * (1.0 + mod[1:2]) + mod[2:3]
    h_ref[0] = h.astype(h_ref.dtype)
    if has_router:
        h_hi = h.astype(BF)
        h_lo = (h - h_hi.astype(F32)).astype(BF)
        lg_ref[0] = _dot(h_hi, r_ref[0]) + (_dot(h_lo, r_ref[0]) + _dot(h_hi, r_ref[1]))


def _resid_norm(x, delta, mod, g, router=None, final=False, proj=None):
    has_delta = delta is not None
    has_router = router is not None
    has_proj = proj is not None
    n_t = S // TOK_TILE
    off = CTX // TOK_TILE if final else 0
    grid = (NB, n_t - off)
    tok = pl.BlockSpec((1, TOK_TILE, D), lambda b, i: (b, i + off, 0))
    in_specs = [tok] + ([pl.BlockSpec((1, TOK_TILE, D), lambda b, i: (b, i, 0))] if has_delta else [])
    in_specs += [pl.BlockSpec((D, D), lambda b, i: (0, 0), pipeline_mode=pl.Buffered(1))] if has_proj else []
    in_specs += [
        pl.BlockSpec((1, 1, 8, D), lambda b, i: (b, jnp.minimum(i + off, 1), 0, 0)),
        pl.BlockSpec((1, D), lambda b, i: (0, 0)),
    ]
    args = [x] + ([delta] if has_delta else []) + ([proj] if has_proj else []) + [mod, g.reshape(1, D)]
    out_specs, out_shape = [], []
    if final:
        out_specs.append(pl.BlockSpec((1, TOK_TILE, D), lambda b, i: (b, i, 0)))
        out_shape.append(jax.ShapeDtypeStruct((NB, T, D), F32))
    else:
        if has_delta:
            out_specs.append(tok)
            out_shape.append(jax.ShapeDtypeStruct((NB, S, D), F32))
        out_specs.append(tok)
        out_shape.append(jax.ShapeDtypeStruct((NB, S, D), F32 if has_router else BF))
    if has_router:
        in_specs.append(pl.BlockSpec((2, D, 128), lambda b, i: (0, 0, 0)))
        r_hi = router.astype(BF)
        args.append(jnp.stack([r_hi, (router - r_hi.astype(F32)).astype(BF)]))
        out_specs.append(pl.BlockSpec((1, TOK_TILE, 128), lambda b, i: (b, i, 0)))
        out_shape.append(jax.ShapeDtypeStruct((NB, S, 128), F32))
    return pl.pallas_call(
        functools.partial(_rn_kernel, has_delta=has_delta, has_proj=has_proj, has_router=has_router, final=final),
        grid=grid,
        in_specs=in_specs,
        out_specs=out_specs,
        out_shape=out_shape,
        compiler_params=_params(("arbitrary", "arbitrary"), 40),
        name="resid_norm",
    )(*args)


def _mm_kernel(a_ref, w_ref, o_ref):
    o_ref[...] = _dot(a_ref[...].astype(BF), w_ref[...].astype(BF)).astype(o_ref.dtype)


def _matmul(a, w, *, tm, tn, out_dtype, a_col_blk=0, k=None, name="matmul"):
    m = a.shape[0]
    k = k or a.shape[1]
    n = w.shape[1]
    return pl.pallas_call(
        _mm_kernel,
        grid=(n // tn, m // tm),
        in_specs=[
            pl.BlockSpec((tm, k), lambda j, i: (i, a_col_blk)),
            pl.BlockSpec((k, tn), lambda j, i: (0, j)),
        ],
        out_specs=pl.BlockSpec((tm, tn), lambda j, i: (i, j)),
        out_shape=jax.ShapeDtypeStruct((m, n), out_dtype),
        compiler_params=_params(("arbitrary", "arbitrary"), 48),
        name=name,
    )(a, w)


def _w_in_kernel(off_ref, a_ref, w_ref, o_ref, wb_ref):
    @pl.when(pl.program_id(1) == 0)
    def _():
        wb_ref[...] = w_ref[0].astype(BF)

    o_ref[...] = _dot(a_ref[...], wb_ref[...]).astype(o_ref.dtype)


def _w_in(h, w, layer):
    tm = 1024
    return pl.pallas_call(
        _w_in_kernel,
        grid_spec=pltpu.PrefetchScalarGridSpec(
            num_scalar_prefetch=1,
            grid=(len(W_IN_SRC), M // tm),
            in_specs=[
                pl.BlockSpec((tm, D), lambda j, i, off: (i, 0)),
                pl.BlockSpec((pl.Element(1), pl.Element(D), pl.Element(W_IN_TILE)),
                             lambda j, i, off: (layer, 0, pl.multiple_of(off[j], 128))),
            ],
            out_specs=pl.BlockSpec((tm, W_IN_TILE), lambda j, i, off: (i, j)),
            scratch_shapes=[pltpu.VMEM((D, W_IN_TILE), BF)],
        ),
        out_shape=jax.ShapeDtypeStruct((M, U_COLS), BF),
        compiler_params=_params(("arbitrary", "arbitrary"), 48),
        name="w_in",
    )(jnp.asarray(W_IN_SRC, jnp.int32), h, w)


def _na_window(r):
    rr = r - CTX // GRID_W
    rs = jnp.clip(rr - WIN_H // 2, 0, ROWS - WIN_H)
    return rr, rs


def _na_kernel(q_ref, k_ref, v_ref, bias_ref, o_ref):
    _, rs = _na_window(pl.program_id(1))
    kstart = pl.multiple_of(CTX + GRID_W * rs, GRID_W)
    lo = lax.broadcasted_iota(jnp.int32, (GRID_W, 128), 1) < HD
    n_nb = WIN_H * GRID_W
    pairs = [slice(128 * p, 128 * p + 128) for p in range(HEADS // 2)]
    rowmax = lambda z: jnp.max(z, axis=-1, keepdims=True)
    rowsum = lambda z: jnp.sum(z, axis=-1, keepdims=True)
    q = [q_ref[0, :, cs].astype(F32) * (HD ** -0.5) for cs in pairs]
    q2 = [jnp.concatenate([jnp.where(lo, z, 0.0), jnp.where(lo, 0.0, z)], axis=0).astype(BF) for z in q]
    s_nb = [_dot_nt(z, k_ref[0, pl.ds(kstart, n_nb), cs]) + bias_ref[0, 128 * p:128 * p + 128, :]
            for p, (z, cs) in enumerate(zip(q2, pairs))]
    s_cx = [_dot_nt(z, k_ref[0, 0:CTX, cs]) for z, cs in zip(q2, pairs)]
    m = [jnp.maximum(rowmax(a), rowmax(b)) for a, b in zip(s_nb, s_cx)]
    e_nb = [jnp.exp(a - mi) for a, mi in zip(s_nb, m)]
    e_cx = [jnp.exp(b - mi) for b, mi in zip(s_cx, m)]
    l = [rowsum(a) + rowsum(b) for a, b in zip(e_nb, e_cx)]
    o2 = [(_dot(a.astype(BF), v_ref[0, pl.ds(kstart, n_nb), cs]) + _dot(b.astype(BF), v_ref[0, 0:CTX, cs])) / li
          for a, b, li, cs in zip(e_nb, e_cx, l, pairs)]
    for z, cs in zip(o2, pairs):
        o_ref[0, :, cs] = jnp.where(lo, z[0:GRID_W], z[GRID_W:]).astype(o_ref.dtype)


def _na_bias_table(rel_bias):
    cols = np.arange(GRID_W)
    col_start = np.clip(cols - WIN_W // 2, 0, GRID_W - WIN_W)
    col_ok = (cols[None, :] >= col_start[:, None]) & (cols[None, :] < col_start[:, None] + WIN_W)
    dc = np.clip(cols[None, :] - cols[:, None] + (WIN_W - 1), 0, 2 * WIN_W - 2)
    pick = (dc[None] == np.arange(2 * WIN_W - 1)[:, None, None]).astype(np.float32)
    by_dr = jnp.einsum("hac,cqk->haqk", rel_bias.astype(F32), pick, precision=HI)
    by_dr = jnp.where(col_ok[None, None], by_dr, NEG)
    tab = jnp.stack([by_dr[:, d0:d0 + WIN_H] for d0 in range(WIN_H)])
    tab = jnp.transpose(tab, (0, 1, 3, 2, 4)).reshape(WIN_H, HEADS * GRID_W, WIN_H * GRID_W)
    return jnp.concatenate([tab, jnp.full((1,) + tab.shape[1:], NEG, F32)], axis=0)


def _attention(u3, bias_tab):
    def bias_idx(b, r):
        rr, rs = _na_window(r)
        return (jnp.where(rr < 0, WIN_H, rs - rr + WIN_H - 1), 0, 0)

    return pl.pallas_call(
        _na_kernel,
        grid=(NB, S // GRID_W),
        in_specs=[
            pl.BlockSpec((1, GRID_W, W), lambda b, r: (b, r, U_NA_Q)),
            pl.BlockSpec((1, S, W), lambda b, r: (b, 0, U_NA_K)),
            pl.BlockSpec((1, S, W), lambda b, r: (b, 0, U_NA_V)),
            pl.BlockSpec((1, HEADS * GRID_W, WIN_H * GRID_W), bias_idx),
        ],
        out_specs=pl.BlockSpec((1, GRID_W, W), lambda b, r: (b, r, 0)),
        out_shape=jax.ShapeDtypeStruct((NB, S, W), BF),
        compiler_params=_params(("arbitrary", "arbitrary"), 48),
        name="attention",
    )(u3, u3, u3, bias_tab)


def _shift_prev(z, halo_row, first_is_start, rev):
    n = z.shape[0]
    row = lax.broadcasted_iota(jnp.int32, z.shape, 0)
    edge = jnp.where(first_is_start, 0.0, halo_row)
    if rev:
        return jnp.where(row == n - 1, edge, pltpu.roll(z, n - 1, 0))
    return jnp.where(row == 0, edge, pltpu.roll(z, 1, 0))


def _rw_prep_kernel(k_ref, v_ref, r_ref, l_ref, kh_ref, vh_ref, rh_ref, lh_ref, vec_ref, w2_ref, a2_ref,
                    ones_ref, lw_o, k_o, v_o, kk_o, b_o, r_o, bonus_o, *, rev):
    i = pl.program_id(1)
    n_t = S // TOK_TILE
    start = (i == n_t - 1) | (i == 0) if rev else (i <= 1)
    hrow = 0 if rev else HALO - 1
    vec = vec_ref[0]
    mu_k, mu_v, mu_r, mu_l, w0, a0, k_k, k_a, r_k = (vec[j:j + 1] for j in range(9))

    def lerp(ref, href, mu):
        z = ref[0].astype(F32)
        prev = _shift_prev(z, href[0].astype(F32)[hrow:hrow + 1, :], start, rev)
        return z + (prev - z) * mu

    lz = lerp(l_ref, lh_ref, mu_l)
    wz = w0 + _dot(jnp.tanh(lz[:, LORA_W:LORA_W + 128]), w2_ref[0], precision=HI)
    a = jax.nn.sigmoid(a0 + _dot(lz[:, LORA_A:LORA_A + 128], a2_ref[0], precision=HI))
    softplus = jnp.maximum(-wz, 0.0) + jnp.log1p(jnp.exp(-jnp.abs(wz)))
    lw_o[0] = -jnp.exp(-softplus - 0.5)
    k1 = lerp(k_ref, kh_ref, mu_k)
    kkr = k1 * k_k
    ones = ones_ref[...]
    nrm = jnp.sqrt(_head_sum(kkr * kkr, ones))
    kk = kkr / jnp.maximum(nrm, 1e-12)
    k2 = k1 * (1.0 + (a - 1.0) * k_a)
    v1 = lerp(v_ref, vh_ref, mu_v)
    r1 = lerp(r_ref, rh_ref, mu_r)
    k_o[0] = k2
    v_o[0] = v1
    kk_o[0] = kk
    b_o[0] = kk * a
    r_o[0] = r1
    bonus_o[0] = _head_sum(r1 * k2 * r_k, ones) * v1


def _rw_prep(u3, ul3, vec, w2e, a2e, ones_bd, d):
    rev = d == 1
    n_t = S // TOK_TILE
    hb = TOK_TILE // HALO

    def tile(blk):
        return pl.BlockSpec((1, TOK_TILE, W), lambda b, i: (b, i, blk))

    def halo(blk):
        if rev:
            return pl.BlockSpec((1, HALO, W), lambda b, i: (b, jnp.minimum((i + 1) * hb, S // HALO - 1), blk))
        return pl.BlockSpec((1, HALO, W), lambda b, i: (b, jnp.maximum(i * hb - 1, 0), blk))

    blks = (U_RW_K, U_RW_V, U_RW_R, 0)
    out = pl.BlockSpec((1, TOK_TILE, W), lambda b, i: (b, i, 0))
    return pl.pallas_call(
        functools.partial(_rw_prep_kernel, rev=rev),
        grid=(NB, n_t),
        in_specs=[tile(x) for x in blks] + [halo(x) for x in blks] + [
            pl.BlockSpec((1, 16, W), lambda b, i: (d, 0, 0)),
            pl.BlockSpec((1, 2 * LORA, W), lambda b, i: (d, 0, 0)),
            pl.BlockSpec((1, 2 * LORA, W), lambda b, i: (d, 0, 0)),
            pl.BlockSpec((W, W), lambda b, i: (0, 0)),
        ],
        out_specs=[out] * 7,
        out_shape=[jax.ShapeDtypeStruct((NB, S, W), F32)] * 7,
        compiler_params=_params(("arbitrary", "arbitrary"), 48),
        name="rw_prep",
    )(u3, u3, u3, ul3, u3, u3, u3, ul3, vec, w2e, a2e, ones_bd)


def _pair_rows(z, lo):
    return jnp.concatenate([jnp.where(lo, z, 0.0), jnp.where(lo, 0.0, z)], axis=0)


def _rw_scan_kernel(*refs):
    ins, y_refs, s_ref = (refs[0:6], refs[6:12]), refs[12:14], refs[14]

    @pl.when(pl.program_id(1) == 0)
    def _():
        s_ref[...] = jnp.zeros_like(s_ref)

    n = CHUNK
    t_i = lax.broadcasted_iota(jnp.int32, (n, n), 0)
    s_i = lax.broadcasted_iota(jnp.int32, (n, n), 1)
    t2 = lax.broadcasted_iota(jnp.int32, (2 * n, 2 * n), 0)
    s2 = lax.broadcasted_iota(jnp.int32, (2 * n, 2 * n), 1)
    same = (t2 >= n) == (s2 >= n)
    tt, ss = t2 & (n - 1), s2 & (n - 1)
    eye2 = t2 == s2
    lo = lax.broadcasted_iota(jnp.int32, (n, 128), 1) < HD
    incl1 = (s_i <= t_i, s_i >= t_i)
    incl2 = (same & (ss <= tt), same & (ss >= tt))
    strict2 = (same & (ss < tt), same & (ss > tt))
    last = (n - 1, 0)
    bf = lambda z: z.astype(BF)

    lw_all = [ins[d][0][0] for d in range(2)]
    cum_all = [_dot(incl1[d].astype(F32), lw_all[d], precision=HI) for d in range(2)]
    chains = [(d, p) for d in range(2) for p in range(HEADS // 2)]

    def operands(d, p):
        cs = slice(128 * p, 128 * p + 128)
        _, k_ref, v_ref, kk_ref, b_ref, r_ref = ins[d]
        lw, cum = lw_all[d][:, cs], cum_all[d][:, cs]
        tot = cum[last[d]:last[d] + 1, :]
        g, gp, gi, gc = jnp.exp(cum), jnp.exp(cum - lw), jnp.exp(-cum), jnp.exp(tot - cum)
        kk, b, k, r, v = kk_ref[0, :, cs], b_ref[0, :, cs], k_ref[0, :, cs], r_ref[0, :, cs], v_ref[0, :, cs]
        return dict(
            a2=bf(_pair_rows(-kk * gp, lo)), r2=_pair_rows(r * g, lo), b2=bf(_pair_rows(b * gi, lo)),
            k2=bf(_pair_rows(k * gi, lo)), bt2=bf(_pair_rows(b * gc, lo)), kt2=bf(_pair_rows(k * gc, lo)),
            v2=bf(_pair_rows(v, lo)), gtot=jnp.exp(tot))

    op = [operands(d, p) for d, p in chains]
    sc = [_dot_nt(jnp.concatenate([o["a2"], bf(o["r2"])], axis=0), jnp.concatenate([o["b2"], o["k2"]], axis=0))
          for o in op]
    m_ab = [jnp.where(strict2[d], s[:2 * n, :2 * n], 0.0) for (d, _), s in zip(chains, sc)]
    m_ak = [bf(jnp.where(strict2[d], s[:2 * n, 2 * n:], 0.0)) for (d, _), s in zip(chains, sc)]
    m_rb = [bf(jnp.where(incl2[d], s[2 * n:, :2 * n], 0.0)) for (d, _), s in zip(chains, sc)]
    m_rk = [bf(jnp.where(incl2[d], s[2 * n:, 2 * n:], 0.0)) for (d, _), s in zip(chains, sc)]
    x = [jnp.where(eye2, 1.0, 0.0) + m for m in m_ab]
    pw = m_ab
    for _ in range(int(np.log2(n)) - 1):
        pw = [_dot(bf(z), bf(z)) for z in pw]
        x = [xi + _dot(bf(xi), bf(z)) for xi, z in zip(x, pw)]
    xb = [bf(z) for z in x]
    mv = [bf(_dot(m, o["v2"])) for m, o in zip(m_ak, op)]
    wmb = [bf(_dot(z, o["a2"])) for z, o in zip(xb, op)]
    ub = [bf(_dot(z, m)) for z, m in zip(xb, mv)]
    rm = [o["r2"] + _dot(m, w) for o, m, w in zip(op, m_rb, wmb)]
    yu = [_dot(mb, u) + _dot(mk, o["v2"]) for mb, mk, u, o in zip(m_rb, m_rk, ub, op)]
    tm = [jnp.where(eye2, o["gtot"], 0.0) + _dot_tn(o["bt2"], w) for o, w in zip(op, wmb)]
    sv = [_dot_tn(o["bt2"], u) + _dot_tn(o["kt2"], o["v2"]) for o, u in zip(op, ub)]
    st = [bf(s_ref[d, p]) for d, p in chains]
    y2 = [_dot(bf(r), s) + y for r, s, y in zip(rm, st, yu)]
    s_new = [_dot(bf(t), s) + v for t, s, v in zip(tm, st, sv)]
    for (d, p), y, s in zip(chains, y2, s_new):
        y_refs[d][0, :, 128 * p:128 * p + 128] = y[:n] + y[n:]
        s_ref[d, p] = s


def _rw_scan(fwd, rev):
    def rev_idx(b, c):
        return (b, jnp.where(c < CTX_CHUNKS, CTX_CHUNKS - 1 - c, N_CHUNK + CTX_CHUNKS - 1 - c), 0)

    f_spec = pl.BlockSpec((1, CHUNK, W), lambda b, c: (b, c, 0))
    r_spec = pl.BlockSpec((1, CHUNK, W), rev_idx)
    return pl.pallas_call(
        _rw_scan_kernel,
        grid=(NB, N_CHUNK),
        in_specs=[f_spec] * 6 + [r_spec] * 6,
        out_specs=[f_spec, r_spec],
        out_shape=[jax.ShapeDtypeStruct((NB, S, W), F32)] * 2,
        scratch_shapes=[pltpu.VMEM((2, HEADS // 2, 128, 128), F32)],
        compiler_params=_params(("arbitrary", "arbitrary"), 32),
        name="rw_scan",
    )(*fwd, *rev)


def _rw_out_kernel(y0_ref, y1_ref, b0_ref, b1_ref, l_ref, g2_ref, avg_ref, ln_ref, o_ref):
    y = y0_ref[0] + y1_ref[0]
    avg = avg_ref[...]
    mu = _head_sum(y, avg)
    yc = y - mu
    var = _head_sum(yc * yc, avg)
    ln = ln_ref[...]
    y = yc * lax.rsqrt(var + GN_EPS) * ln[0:1] + ln[1:2] + b0_ref[0] + b1_ref[0]
    gate = _dot(jax.nn.sigmoid(l_ref[0].astype(F32)).astype(BF), g2_ref[...])
    o_ref[0] = (y * gate).astype(o_ref.dtype)


def _rw_out(y0, y1, bon0, bon1, ul3, g2e, avg_bd, ln):
    tile = pl.BlockSpec((1, TOK_TILE, W), lambda b, i: (b, i, 0))
    return pl.pallas_call(
        _rw_out_kernel,
        grid=(NB, S // TOK_TILE),
        in_specs=[tile] * 4 + [
            tile,
            pl.BlockSpec((W, W), lambda b, i: (0, 0)),
            pl.BlockSpec((W, W), lambda b, i: (0, 0)),
            pl.BlockSpec((8, W), lambda b, i: (0, 0)),
        ],
        out_specs=tile,
        out_shape=jax.ShapeDtypeStruct((NB, S, W), BF),
        compiler_params=_params(("arbitrary", "arbitrary"), 32),
        name="rw_out",
    )(y0, y1, bon0, bon1, ul3, g2e, avg_bd, ln)


def _dft_mats():
    gd = W // 4
    cc = np.arange(gd)
    ang = 2 * np.pi * ((cc[:, None] * cc[None, :]) % gd) / gd
    chan = np.zeros((W, 2 * W), np.float32)
    for g in range(4):
        chan[g * gd:(g + 1) * gd, g * gd:(g + 1) * gd] = np.cos(ang)
        chan[g * gd:(g + 1) * gd, W + g * gd:W + (g + 1) * gd] = np.sin(ang)
    cos = np.zeros((S, S), np.float32)
    nsin = np.zeros((S, S), np.float32)
    scale = np.zeros((S, 1), np.float32)
    for lo, n in ((0, CTX), (CTX, T)):
        tt = np.arange(n)
        a = 2 * np.pi * ((tt[:, None] * tt[None, :]) % n) / n
        cos[lo:lo + n, lo:lo + n] = np.cos(a)
        nsin[lo:lo + n, lo:lo + n] = -np.sin(a)
        scale[lo:lo + n] = (n * gd) ** -0.5
    return chan, cos, nsin, scale


def _fn_tok_kernel(cos_ref, nsin_ref, z_ref, sc_ref, o_ref):
    z = z_ref[0]
    f = _dot(cos_ref[...], z[:, 0:W]) + _dot(nsin_ref[...], z[:, W:2 * W])
    o_ref[0] = (f * sc_ref[...]).astype(o_ref.dtype)


def _fourier(u2, chan, cos, nsin, scale):
    zcs = _matmul(u2, chan, tm=1024, tn=2 * W, out_dtype=BF, a_col_blk=U_FN_X, k=W, name="fn_chan")
    return pl.pallas_call(
        _fn_tok_kernel,
        grid=(NB, S // TOK_TILE),
        in_specs=[
            pl.BlockSpec((TOK_TILE, S), lambda b, i: (i, 0)),
            pl.BlockSpec((TOK_TILE, S), lambda b, i: (i, 0)),
            pl.BlockSpec((1, S, 2 * W), lambda b, i: (b, 0, 0)),
            pl.BlockSpec((TOK_TILE, 1), lambda b, i: (i, 0)),
        ],
        out_specs=pl.BlockSpec((1, TOK_TILE, W), lambda b, i: (b, i, 0)),
        out_shape=jax.ShapeDtypeStruct((NB, S, W), BF),
        compiler_params=_params(("arbitrary", "arbitrary"), 40),
        name="fn_tok",
    )(cos, nsin, zcs.reshape(NB, S, 2 * W), scale)


def _sc_kernel(b_ref, c_ref, x_ref, cp_ref, xp_ref, cn_ref, xn_ref, w_ref, o_ref):
    i = pl.program_id(1)
    n_t = S // TOK_TILE
    f32 = lambda ref: ref[0].astype(F32)
    z = f32(c_ref) * f32(x_ref)
    zp = _shift_prev(z, (f32(cp_ref) * f32(xp_ref))[HALO - 1:HALO, :], i <= 1, False)
    zn = _shift_prev(z, (f32(cn_ref) * f32(xn_ref))[0:1, :], (i == 0) | (i == n_t - 1), True)
    w = w_ref[...]
    o_ref[0] = (f32(b_ref) * (zp * w[0:1] + z * w[1:2] + zn * w[2:3])).astype(o_ref.dtype)


def _short_conv(u3, conv_w):
    hb = TOK_TILE // HALO

    def tile(blk):
        return pl.BlockSpec((1, TOK_TILE, W), lambda b, i: (b, i, blk))

    def prev(blk):
        return pl.BlockSpec((1, HALO, W), lambda b, i: (b, jnp.maximum(i * hb - 1, 0), blk))

    def nxt(blk):
        return pl.BlockSpec((1, HALO, W), lambda b, i: (b, jnp.minimum((i + 1) * hb, S // HALO - 1), blk))

    return pl.pallas_call(
        _sc_kernel,
        grid=(NB, S // TOK_TILE),
        in_specs=[tile(U_SC_B), tile(U_SC_C), tile(U_SC_X), prev(U_SC_C), prev(U_SC_X), nxt(U_SC_C), nxt(U_SC_X),
                  pl.BlockSpec((8, W), lambda b, i: (0, 0))],
        out_specs=pl.BlockSpec((1, TOK_TILE, W), lambda b, i: (b, i, 0)),
        out_shape=jax.ShapeDtypeStruct((NB, S, W), BF),
        compiler_params=_params(("arbitrary", "arbitrary"), 32),
        name="short_conv",
    )(*([u3] * 7), conv_w)


def _merge_kernel(y0_ref, y1_ref, y2_ref, y3_ref, g0_ref, g1_ref, g2_ref, g3_ref, w_ref, o_ref):
    acc = None
    for n, (y_ref, g_ref) in enumerate(((y0_ref, g0_ref), (y1_ref, g1_ref), (y2_ref, g2_ref), (y3_ref, g3_ref))):
        gate = 0.5 * jnp.tanh(0.5 * g_ref[...].astype(F32)) + 0.5
        t = gate * _dot(y_ref[...], w_ref[n])
        acc = t if acc is None else acc + t
    o_ref[...] = acc.astype(o_ref.dtype)


def _merge(branches, u2, w_branch):
    tm, tn = 512, 512

    def gate(n):
        return pl.BlockSpec((tm, tn), lambda i, j: (i, U_GATES + n * (D // tn) + j))

    return pl.pallas_call(
        _merge_kernel,
        grid=(M // tm, D // tn),
        in_specs=[pl.BlockSpec((tm, W), lambda i, j: (i, 0))] * 4 + [gate(n) for n in range(4)] + [
            pl.BlockSpec((4, W, tn), lambda i, j: (0, 0, j))],
        out_specs=pl.BlockSpec((tm, tn), lambda i, j: (i, j)),
        out_shape=jax.ShapeDtypeStruct((M, D), BF),
        compiler_params=_params(("arbitrary", "arbitrary"), 32),
        name="merge",
    )(*branches, u2, u2, u2, u2, w_branch)


def _ffn_kernel(te_ref, nv_ref, x_ref, w1_ref, w3_ref, w2_ref, o_ref):
    i, f = pl.program_id(0), pl.program_id(1)

    @pl.when(f == 0)
    def _():
        o_ref[...] = jnp.zeros_like(o_ref)

    @pl.when(i < nv_ref[0])
    def _():
        x = x_ref[...]
        a = _dot(x, w1_ref[0].astype(BF))
        b = _dot(x, w3_ref[0].astype(BF))
        g = (a * jax.nn.sigmoid(a) * b).astype(BF)
        o_ref[...] += _dot(g, w2_ref[0].astype(BF))


def _ffn(x, w1, w3, w2, tile_expert, n_valid, *, tm, tf):
    rows = x.shape[0]
    n_tiles = rows // tm

    n_f = D_FF // tf

    def tile(i, nv):
        return jnp.minimum(i, nv[0] - 1)

    def ff(i, f, nv):
        return jnp.where(i < nv[0], f, n_f - 1)

    once = dict(pipeline_mode=pl.Buffered(1))
    return pl.pallas_call(
        _ffn_kernel,
        grid_spec=pltpu.PrefetchScalarGridSpec(
            num_scalar_prefetch=2,
            grid=(n_tiles, n_f),
            in_specs=[
                pl.BlockSpec((tm, D), lambda i, f, te, nv: (tile(i, nv), 0), **once),
                pl.BlockSpec((1, D, tf), lambda i, f, te, nv: (te[tile(i, nv)], 0, ff(i, f, nv))),
                pl.BlockSpec((1, D, tf), lambda i, f, te, nv: (te[tile(i, nv)], 0, ff(i, f, nv))),
                pl.BlockSpec((1, tf, D), lambda i, f, te, nv: (te[tile(i, nv)], ff(i, f, nv), 0)),
            ],
            out_specs=pl.BlockSpec((tm, D), lambda i, f, te, nv: (i, 0), **once),
        ),
        out_shape=jax.ShapeDtypeStruct((rows, D), F32),
        compiler_params=_params(("arbitrary", "arbitrary"), 56),
        name="ffn",
    )(tile_expert, n_valid, x, w1, w3, w2)


def _gather_kernel(idx_ref, live_ref, src_ref, *rest, n_slots, n_rows, weighted):
    wt_ref = rest[0] if weighted else None
    o_ref, buf, sem = rest[-3:]
    g, n_live = pl.program_id(0), live_ref[0]

    def issue(step, slot):
        def body(k, carry):
            for s in range(n_slots):
                row = idx_ref[s * n_rows + step * GATHER_ROWS + k]
                pltpu.make_async_copy(src_ref.at[pl.ds(row, 1)], buf.at[slot, pl.ds(s * GATHER_ROWS + k, 1)],
                                      sem.at[slot]).start()
            return carry
        lax.fori_loop(0, GATHER_ROWS, body, 0, unroll=8)

    @pl.when(g == 0)
    def _():
        issue(0, 0)

    @pl.when(g + 1 < n_live)
    def _():
        issue(g + 1, (g + 1) % 2)

    @pl.when(g < n_live)
    def _():
        slot = g % 2
        pltpu.make_async_copy(src_ref.at[pl.ds(0, n_slots * GATHER_ROWS)], buf.at[slot], sem.at[slot]).wait()
        acc = None
        for s in range(n_slots):
            rows = buf[slot, s * GATHER_ROWS:(s + 1) * GATHER_ROWS]
            if wt_ref is not None:
                rows = rows * wt_ref[:, s:s + 1]
            acc = rows if acc is None else acc + rows
        o_ref[...] = acc.astype(o_ref.dtype)

    @pl.when(g >= n_live)
    def _():
        o_ref[...] = jnp.zeros_like(o_ref)


def _row_gather(src, idx, n_slots, out_dtype, n_live_rows=None, weights=None):
    n_rows = idx.shape[0] // n_slots
    n_steps = n_rows // GATHER_ROWS
    live = jnp.full((1,), n_steps, jnp.int32) if n_live_rows is None else n_live_rows // GATHER_ROWS
    weighted = weights is not None
    wt_spec = [pl.BlockSpec((GATHER_ROWS, n_slots), lambda g, idx, live: (g, 0))] if weighted else []
    return pl.pallas_call(
        functools.partial(_gather_kernel, n_slots=n_slots, n_rows=n_rows, weighted=weighted),
        grid_spec=pltpu.PrefetchScalarGridSpec(
            num_scalar_prefetch=2,
            grid=(n_steps,),
            in_specs=[pl.BlockSpec(memory_space=pl.ANY)] + wt_spec,
            out_specs=pl.BlockSpec((GATHER_ROWS, D), lambda g, idx, live: (g, 0)),
            scratch_shapes=[pltpu.VMEM((2, n_slots * GATHER_ROWS, D), F32), pltpu.SemaphoreType.DMA((2,))],
        ),
        out_shape=jax.ShapeDtypeStruct((n_rows, D), out_dtype),
        compiler_params=_params(("arbitrary",), 32),
        name="row_gather",
    )(idx, live, src, *([weights] if weighted else []))


def _moe(hf, logits, tok, w1, w3, w2, layer):
    n_tok = tok.shape[0]
    n_tiles = (2 * n_tok) // MOE_TM + N_EXP
    top_val, top_idx = lax.top_k(logits, 2)
    top_p = jax.nn.softmax(top_val, axis=-1)
    e_flat = top_idx.reshape(-1)
    onehot = (e_flat[:, None] == jnp.arange(N_EXP)[None, :]).astype(jnp.int32)
    rank = jnp.take_along_axis(jnp.cumsum(onehot, axis=0) - onehot, e_flat[:, None], axis=1)[:, 0]
    count = jnp.sum(onehot, axis=0)
    tiles_per = (count + MOE_TM - 1) // MOE_TM
    tile_end = jnp.cumsum(tiles_per)
    group_start = (tile_end - tiles_per) * MOE_TM
    pos = group_start[e_flat] + rank
    n_valid = tile_end[-1:].astype(jnp.int32)
    tile_expert = jnp.minimum(jnp.searchsorted(tile_end, jnp.arange(n_tiles), side="right"),
                              N_EXP - 1).astype(jnp.int32)
    rows = n_tiles * MOE_TM
    src = jnp.zeros((rows,), jnp.int32).at[pos].set(jnp.repeat(tok, 2))
    xs = _row_gather(hf, src, 1, BF, n_live_rows=n_valid * MOE_TM)
    ys = _ffn(xs, w1, w3, w2, tile_expert + layer * N_EXP, n_valid, tm=MOE_TM, tf=FF_TILE)
    return _row_gather(ys, pos.reshape(n_tok, 2).T.reshape(-1), 2, F32, weights=top_p)


def _low_rank_w_in(w, layer):
    return jnp.concatenate([w[layer, :, 2048:2304], w[layer, :, 3328:3456], jnp.zeros((D, 128), w.dtype)], axis=1)


def _rows_at(mat, off, n=W):
    return jnp.zeros((n, W), F32).at[off:off + mat.shape[0]].set(mat)


def _pad_rows(rows, n):
    z = jnp.stack(rows)
    return jnp.concatenate([z, jnp.zeros((n - z.shape[0],) + z.shape[1:], z.dtype)], axis=0)


def kernel(x, c, ctx, c_ctx, ada_w, ada_b, norm_mix, norm_ffn, w_in, na_rel_bias, rw_mu_kvr, rw_mu_lora, rw_w0, rw_w2, rw_a0, rw_a2, rw_k_k, rw_k_a, rw_r_k, rw_g2, rw_ln_w, rw_ln_b, sc_conv, w_branch, w_out, ffn_w1, ffn_w3, ffn_w2, moe_router, moe_w1, moe_w3, moe_w2, norm_final):
    xs = jnp.concatenate([ctx, x], axis=1)
    cvec = jnp.concatenate([c, c_ctx[None], jnp.zeros((3, D), F32)], axis=0)
    mods = _ada(cvec, ada_w, ada_b).reshape(DEPTH, 8, 6, D)

    def mod_table(gate, scale, shift):
        def pick(sel):
            if sel is None:
                return jnp.zeros((NB, 2, D), F32)
            v = mods[sel[0], :, sel[1]]
            return jnp.stack([jnp.broadcast_to(v[4], (NB, D)), v[:NB]], axis=1)
        t = jnp.stack([pick(gate), pick(scale), pick(shift)], axis=2)
        return jnp.concatenate([t, jnp.zeros((NB, 2, 5, D), F32)], axis=2)

    head_id = np.arange(W) // HD
    ones_bd = jnp.asarray((head_id[:, None] == head_id[None, :]).astype(np.float32), BF)
    chan, cos, nsin, fscale = _dft_mats()
    chan, cos, nsin, fscale = jnp.asarray(chan, BF), jnp.asarray(cos, BF), jnp.asarray(nsin, BF), jnp.asarray(fscale)

    delta = None
    for l in range(DEPTH):
        mod = mod_table(None if l == 0 else (l - 1, 5), (l, 1), (l, 0))
        if l == 0:
            (h,) = _resid_norm(xs, None, mod, norm_mix[l])
        else:
            xs, h = _resid_norm(xs, delta, mod, norm_mix[l])
        u2 = _w_in(h.reshape(M, D), w_in, l)
        u3 = u2.reshape(NB, S, U_COLS)
        ul3 = _matmul(h.reshape(M, D), _low_rank_w_in(w_in, l), tm=1024, tn=W, out_dtype=BF,
                      name="w_in_low_rank").reshape(NB, S, W)

        y_na = _attention(u3, _na_bias_table(na_rel_bias[l]))

        lora_mu = jnp.zeros((2, W), F32)
        lora_mu = lora_mu.at[:, LORA_W:LORA_W + 2 * LORA].set(
            jnp.broadcast_to(rw_mu_lora[l][:, 0].reshape(1, 2 * LORA), (2, 2 * LORA)))
        lora_mu = lora_mu.at[:, LORA_A:LORA_A + 2 * LORA].set(
            jnp.broadcast_to(rw_mu_lora[l][:, 1].reshape(1, 2 * LORA), (2, 2 * LORA)))
        vec = jnp.stack([_pad_rows([rw_mu_kvr[l][d, 0], rw_mu_kvr[l][d, 1], rw_mu_kvr[l][d, 2], lora_mu[d],
                                    rw_w0[l][d], rw_a0[l][d], rw_k_k[l], rw_k_a[l], rw_r_k[l]], 16)
                         for d in range(2)])
        w2e = jnp.stack([_rows_at(rw_w2[l][d], LORA * d, 2 * LORA) for d in range(2)])
        a2e = jnp.stack([_rows_at(rw_a2[l][d], LORA * d, 2 * LORA) for d in range(2)])
        seqs = [_rw_prep(u3, ul3, vec, w2e, a2e, ones_bd, d) for d in range(2)]
        ys = _rw_scan(seqs[0][:6], seqs[1][:6])
        bons = [seqs[0][6], seqs[1][6]]
        y_rw = _rw_out(ys[0], ys[1], bons[0], bons[1], ul3, _rows_at(rw_g2[l], LORA_G).astype(BF),
                       ones_bd / HD, _pad_rows([rw_ln_w[l], rw_ln_b[l]], 8))

        y_fn = _fourier(u2, chan, cos, nsin, fscale)
        y_sc = _short_conv(u3, _pad_rows([sc_conv[l][0], sc_conv[l][1], sc_conv[l][2]], 8))

        merged = _merge([z.reshape(M, W) for z in (y_na, y_rw, y_fn, y_sc)], u2, w_branch[l].astype(BF))
        merged = merged.reshape(NB, S, D)
        w_o = w_out[l].astype(BF)

        mod = mod_table((l, 2), (l, 4), (l, 3))
        i = l // 2
        last = l == DEPTH - 1
        if l % 2 == 0:
            xs, hf = _resid_norm(xs, merged, mod, norm_ffn[l], proj=w_o)
            delta = _ffn(hf.reshape(M, D), ffn_w1, ffn_w3, ffn_w2,
                         jnp.full((M // 1024,), i, jnp.int32), jnp.full((1,), M // 1024, jnp.int32),
                         tm=1024, tf=FF_TILE).reshape(NB, S, D)
            delta = delta[:, CTX:] if last else delta
        else:
            router = jnp.concatenate([moe_router[i], jnp.zeros((D, 128 - N_EXP), F32)], axis=1)
            xs, hf, logits = _resid_norm(xs, merged, mod, norm_ffn[l], router=router, proj=w_o)
            n_moe = moe_w1.shape[0]
            first = CTX if last else 0
            tok = jnp.arange(M, dtype=jnp.int32).reshape(NB, S)[:, first:].reshape(-1)
            delta = _moe(hf.reshape(M, D), logits[:, first:, :N_EXP].reshape(-1, N_EXP), tok,
                         moe_w1.reshape(n_moe * N_EXP, D, D_FF),
                         moe_w3.reshape(n_moe * N_EXP, D, D_FF), moe_w2.reshape(n_moe * N_EXP, D_FF, D), i)
            delta = delta.reshape(NB, -1, D)

    mod = mod_table((DEPTH - 1, 5), None, None)
    (out,) = _resid_norm(xs, delta, mod, norm_final, final=True)
    return out
```

```python
import functools

import jax
import jax.numpy as jnp
import numpy as np
from jax import lax
from jax.experimental import pallas as pl
from jax.experimental.pallas import tpu as pltpu

BF = jnp.bfloat16
F32 = jnp.float32
HI = lax.Precision.HIGHEST

D = 2048
NB = 4
T = 2048
CTX = 256
S = CTX + T
M = NB * S
DEPTH = 4
GRID_W = 64
ROWS = T // GRID_W
WIN_H = 8
WIN_W = 16
HEADS = 8
HD = 64
W = HEADS * HD
LORA = 64
D_FF = 5632
N_EXP = 8
RMS_EPS = 1e-6
GN_EPS = 64e-5
NEG = -1e30

U_NA_K, U_NA_V, U_RW_K, U_RW_V, U_NA_Q, U_RW_R, U_FN_X, U_SC_B, U_SC_C, U_SC_X, U_GATES = range(11)
W_IN_TILE = 1024
W_IN_SRC = (0, 1024, 2304, 3456, 4480) + tuple(5504 + W_IN_TILE * t for t in range(8))
U_COLS = W_IN_TILE * len(W_IN_SRC)
LORA_W, LORA_A, LORA_G = 0, 128, 256

CHUNK = 64
N_CHUNK = S // CHUNK
CTX_CHUNKS = CTX // CHUNK
TOK_TILE = 256
MOE_TM = 1024
GATHER_ROWS = 256
FF_TILE = 512
HALO = 16


def _params(sem, vmem_mb):
    return pltpu.CompilerParams(dimension_semantics=sem, vmem_limit_bytes=vmem_mb << 20)


def _dot(a, b, **kw):
    return jnp.dot(a, b, preferred_element_type=F32, **kw)


def _dot_nt(a, b):
    return lax.dot_general(a, b, (((1,), (1,)), ((), ())), preferred_element_type=F32)


def _dot_tn(a, b):
    return lax.dot_general(a, b, (((0,), (0,)), ((), ())), preferred_element_type=F32)


def _head_sum(z, sel):
    hi = z.astype(BF)
    lo = (z - hi.astype(F32)).astype(BF)
    return _dot(hi, sel) + _dot(lo, sel)


def _ada_kernel(c_ref, w_ref, b_ref, o_ref):
    c = c_ref[...]
    a = (c * jax.nn.sigmoid(c)).astype(BF)
    o_ref[0] = _dot(a, w_ref[0].astype(BF)) + b_ref[0]


def _ada(cvec, ada_w, ada_b):
    tn = 1024
    return pl.pallas_call(
        _ada_kernel,
        grid=(DEPTH, 6 * D // tn),
        in_specs=[
            pl.BlockSpec((8, D), lambda l, j: (0, 0)),
            pl.BlockSpec((1, D, tn), lambda l, j: (l, 0, j)),
            pl.BlockSpec((1, 1, tn), lambda l, j: (l, 0, j)),
        ],
        out_specs=pl.BlockSpec((1, 8, tn), lambda l, j: (l, 0, j)),
        out_shape=jax.ShapeDtypeStruct((DEPTH, 8, 6 * D), F32),
        compiler_params=_params(("arbitrary", "arbitrary"), 40),
        name="ada",
    )(cvec, ada_w, ada_b.reshape(DEPTH, 1, 6 * D))


def _rn_kernel(*refs, has_delta, has_proj, has_router, final):
    it = iter(refs)
    x_ref = next(it)
    d_ref = next(it) if has_delta else None
    p_ref = next(it) if has_proj else None
    mod_ref = next(it)
    g_ref = next(it)
    r_ref = next(it) if has_router else None
    xo_ref = next(it) if (has_delta and not final) else None
    h_ref = next(it)
    lg_ref = next(it) if has_router else None
    x = x_ref[0]
    mod = mod_ref[0, 0]
    if has_delta:
        d = _dot(d_ref[0], p_ref[...]) if has_proj else d_ref[0]
        x = x + mod[0:1] * d
        if xo_ref is not None:
            xo_ref[0] = x
    y = x * lax.rsqrt(jnp.mean(x * x, axis=-1, keepdims=True) + RMS_EPS) * g_ref[...]
    if final:
        h_ref[0] = y
        return
    h = y * (1.0 + mod[1:2]) + mod[2:3]
    h_ref[0] = h.astype(h_ref.dtype)
    if has_router:
        h_hi = h.astype(BF)
        h_lo = (h - h_hi.astype(F32)).astype(BF)
        lg_ref[0] = _dot(h_hi, r_ref[0]) + (_dot(h_lo, r_ref[0]) + _dot(h_hi, r_ref[1]))


def _resid_norm(x, delta, mod, g, router=None, final=False, proj=None):
    has_delta = delta is not None
    has_router = router is not None
    has_proj = proj is not None
    n_t = S // TOK_TILE
    off = CTX // TOK_TILE if final else 0
    grid = (NB, n_t - off)
    tok = pl.BlockSpec((1, TOK_TILE, D), lambda b, i: (b, i + off, 0))
    in_specs = [tok] + ([pl.BlockSpec((1, TOK_TILE, D), lambda b, i: (b, i, 0))] if has_delta else [])
    in_specs += [pl.BlockSpec((D, D), lambda b, i: (0, 0), pipeline_mode=pl.Buffered(1))] if has_proj else []
    in_specs += [
        pl.BlockSpec((1, 1, 8, D), lambda b, i: (b, jnp.minimum(i + off, 1), 0, 0)),
        pl.BlockSpec((1, D), lambda b, i: (0, 0)),
    ]
    args = [x] + ([delta] if has_delta else []) + ([proj] if has_proj else []) + [mod, g.reshape(1, D)]
    out_specs, out_shape = [], []
    if final:
        out_specs.append(pl.BlockSpec((1, TOK_TILE, D), lambda b, i: (b, i, 0)))
        out_shape.append(jax.ShapeDtypeStruct((NB, T, D), F32))
    else:
        if has_delta:
            out_specs.append(tok)
            out_shape.append(jax.ShapeDtypeStruct((NB, S, D), F32))
        out_specs.append(tok)
        out_shape.append(jax.ShapeDtypeStruct((NB, S, D), F32 if has_router else BF))
    if has_router:
        in_specs.append(pl.BlockSpec((2, D, 128), lambda b, i: (0, 0, 0)))
        r_hi = router.astype(BF)
        args.append(jnp.stack([r_hi, (router - r_hi.astype(F32)).astype(BF)]))
        out_specs.append(pl.BlockSpec((1, TOK_TILE, 128), lambda b, i: (b, i, 0)))
        out_shape.append(jax.ShapeDtypeStruct((NB, S, 128), F32))
    return pl.pallas_call(
        functools.partial(_rn_kernel, has_delta=has_delta, has_proj=has_proj, has_router=has_router, final=final),
        grid=grid,
        in_specs=in_specs,
        out_specs=out_specs,
        out_shape=out_shape,
        compiler_params=_params(("arbitrary", "arbitrary"), 40),
        name="resid_norm",
    )(*args)


def _mm_kernel(a_ref, w_ref, o_ref):
    o_ref[...] = _dot(a_ref[...].astype(BF), w_ref[...].astype(BF)).astype(o_ref.dtype)


def _matmul(a, w, *, tm, tn, out_dtype, a_col_blk=0, k=None, name="matmul"):
    m = a.shape[0]
    k = k or a.shape[1]
    n = w.shape[1]
    return pl.pallas_call(
        _mm_kernel,
        grid=(n // tn, m // tm),
        in_specs=[
            pl.BlockSpec((tm, k), lambda j, i: (i, a_col_blk)),
            pl.BlockSpec((k, tn), lambda j, i: (0, j)),
        ],
        out_specs=pl.BlockSpec((tm, tn), lambda j, i: (i, j)),
        out_shape=jax.ShapeDtypeStruct((m, n), out_dtype),
        compiler_params=_params(("arbitrary", "arbitrary"), 48),
        name=name,
    )(a, w)


def _w_in_kernel(off_ref, a_ref, w_ref, o_ref, wb_ref):
    @pl.when(pl.program_id(1) == 0)
    def _():
        wb_ref[...] = w_ref[0].astype(BF)

    o_ref[...] = _dot(a_ref[...], wb_ref[...]).astype(o_ref.dtype)


def _w_in(h, w, layer):
    tm = 1024
    return pl.pallas_call(
        _w_in_kernel,
        grid_spec=pltpu.PrefetchScalarGridSpec(
            num_scalar_prefetch=1,
            grid=(len(W_IN_SRC), M // tm),
            in_specs=[
                pl.BlockSpec((tm, D), lambda j, i, off: (i, 0)),
                pl.BlockSpec((pl.Element(1), pl.Element(D), pl.Element(W_IN_TILE)),
                             lambda j, i, off: (layer, 0, pl.multiple_of(off[j], 128))),
            ],
            out_specs=pl.BlockSpec((tm, W_IN_TILE), lambda j, i, off: (i, j)),
            scratch_shapes=[pltpu.VMEM((D, W_IN_TILE), BF)],
        ),
        out_shape=jax.ShapeDtypeStruct((M, U_COLS), BF),
        compiler_params=_params(("arbitrary", "arbitrary"), 48),
        name="w_in",
    )(jnp.asarray(W_IN_SRC, jnp.int32), h, w)


def _na_window(r):
    rr = r - CTX // GRID_W
    rs = jnp.clip(rr - WIN_H // 2, 0, ROWS - WIN_H)
    return rr, rs


def _na_kernel(q_ref, k_ref, v_ref, bias_ref, o_ref):
    _, rs = _na_window(pl.program_id(1))
    kstart = pl.multiple_of(CTX + GRID_W * rs, GRID_W)
    lo = lax.broadcasted_iota(jnp.int32, (GRID_W, 128), 1) < HD
    n_nb = WIN_H * GRID_W
    pairs = [slice(128 * p, 128 * p + 128) for p in range(HEADS // 2)]
    rowmax = lambda z: jnp.max(z, axis=-1, keepdims=True)
    rowsum = lambda z: jnp.sum(z, axis=-1, keepdims=True)
    q = [q_ref[0, :, cs].astype(F32) * (HD ** -0.5) for cs in pairs]
    q2 = [jnp.concatenate([jnp.where(lo, z, 0.0), jnp.where(lo, 0.0, z)], axis=0).astype(BF) for z in q]
    s_nb = [_dot_nt(z, k_ref[0, pl.ds(kstart, n_nb), cs]) + bias_ref[0, 128 * p:128 * p + 128, :]
            for p, (z, cs) in enumerate(zip(q2, pairs))]
    s_cx = [_dot_nt(z, k_ref[0, 0:CTX, cs]) for z, cs in zip(q2, pairs)]
    m = [jnp.maximum(rowmax(a), rowmax(b)) for a, b in zip(s_nb, s_cx)]
    e_nb = [jnp.exp(a - mi) for a, mi in zip(s_nb, m)]
    e_cx = [jnp.exp(b - mi) for b, mi in zip(s_cx, m)]
    l = [rowsum(a) + rowsum(b) for a, b in zip(e_nb, e_cx)]
    o2 = [(_dot(a.astype(BF), v_ref[0, pl.ds(kstart, n_nb), cs]) + _dot(b.astype(BF), v_ref[0, 0:CTX, cs])) / li
          for a, b, li, cs in zip(e_nb, e_cx, l, pairs)]
    for z, cs in zip(o2, pairs):
        o_ref[0, :, cs] = jnp.where(lo, z[0:GRID_W], z[GRID_W:]).astype(o_ref.dtype)


def _na_bias_table(rel_bias):
    cols = np.arange(GRID_W)
    col_start = np.clip(cols - WIN_W // 2, 0, GRID_W - WIN_W)
    col_ok = (cols[None, :] >= col_start[:, None]) & (cols[None, :] < col_start[:, None] + WIN_W)
    dc = np.clip(cols[None, :] - cols[:, None] + (WIN_W - 1), 0, 2 * WIN_W - 2)
    pick = (dc[None] == np.arange(2 * WIN_W - 1)[:, None, None]).astype(np.float32)
    by_dr = jnp.einsum("hac,cqk->haqk", rel_bias.astype(F32), pick, precision=HI)
    by_dr = jnp.where(col_ok[None, None], by_dr, NEG)
    tab = jnp.stack([by_dr[:, d0:d0 + WIN_H] for d0 in range(WIN_H)])
    tab = jnp.transpose(tab, (0, 1, 3, 2, 4)).reshape(WIN_H, HEADS * GRID_W, WIN_H * GRID_W)
    return jnp.concatenate([tab, jnp.full((1,) + tab.shape[1:], NEG, F32)], axis=0)


def _attention(u3, bias_tab):
    def bias_idx(b, r):
        rr, rs = _na_window(r)
        return (jnp.where(rr < 0, WIN_H, rs - rr + WIN_H - 1), 0, 0)

    return pl.pallas_call(
        _na_kernel,
        grid=(NB, S // GRID_W),
        in_specs=[
            pl.BlockSpec((1, GRID_W, W), lambda b, r: (b, r, U_NA_Q)),
            pl.BlockSpec((1, S, W), lambda b, r: (b, 0, U_NA_K)),
            pl.BlockSpec((1, S, W), lambda b, r: (b, 0, U_NA_V)),
            pl.BlockSpec((1, HEADS * GRID_W, WIN_H * GRID_W), bias_idx),
        ],
        out_specs=pl.BlockSpec((1, GRID_W, W), lambda b, r: (b, r, 0)),
        out_shape=jax.ShapeDtypeStruct((NB, S, W), BF),
        compiler_params=_params(("arbitrary", "arbitrary"), 48),
        name="attention",
    )(u3, u3, u3, bias_tab)


def _shift_prev(z, halo_row, first_is_start, rev):
    n = z.shape[0]
    row = lax.broadcasted_iota(jnp.int32, z.shape, 0)
    edge = jnp.where(first_is_start, 0.0, halo_row)
    if rev:
        return jnp.where(row == n - 1, edge, pltpu.roll(z, n - 1, 0))
    return jnp.where(row == 0, edge, pltpu.roll(z, 1, 0))


def _rw_prep_kernel(k_ref, v_ref, r_ref, l_ref, kh_ref, vh_ref, rh_ref, lh_ref, vec_ref, w2_ref, a2_ref,
                    ones_ref, lw_o, k_o, v_o, kk_o, b_o, r_o, bonus_o, *, rev):
    i = pl.program_id(1)
    n_t = S // TOK_TILE
    start = (i == n_t - 1) | (i == 0) if rev else (i <= 1)
    hrow = 0 if rev else HALO - 1
    vec = vec_ref[0]
    mu_k, mu_v, mu_r, mu_l, w0, a0, k_k, k_a, r_k = (vec[j:j + 1] for j in range(9))

    def lerp(ref, href, mu):
        z = ref[0].astype(F32)
        prev = _shift_prev(z, href[0].astype(F32)[hrow:hrow + 1, :], start, rev)
        return z + (prev - z) * mu

    lz = lerp(l_ref, lh_ref, mu_l)
    wz = w0 + _dot(jnp.tanh(lz[:, LORA_W:LORA_W + 128]), w2_ref[0], precision=HI)
    a = jax.nn.sigmoid(a0 + _dot(lz[:, LORA_A:LORA_A + 128], a2_ref[0], precision=HI))
    softplus = jnp.maximum(-wz, 0.0) + jnp.log1p(jnp.exp(-jnp.abs(wz)))
    lw_o[0] = -jnp.exp(-softplus - 0.5)
    k1 = lerp(k_ref, kh_ref, mu_k)
    kkr = k1 * k_k
    ones = ones_ref[...]
    nrm = jnp.sqrt(_head_sum(kkr * kkr, ones))
    kk = kkr / jnp.maximum(nrm, 1e-12)
    k2 = k1 * (1.0 + (a - 1.0) * k_a)
    v1 = lerp(v_ref, vh_ref, mu_v)
    r1 = lerp(r_ref, rh_ref, mu_r)
    k_o[0] = k2
    v_o[0] = v1
    kk_o[0] = kk
    b_o[0] = kk * a
    r_o[0] = r1
    bonus_o[0] = _head_sum(r1 * k2 * r_k, ones) * v1


def _rw_prep(u3, ul3, vec, w2e, a2e, ones_bd, d):
    rev = d == 1
    n_t = S // TOK_TILE
    hb = TOK_TILE // HALO

    def tile(blk):
        return pl.BlockSpec((1, TOK_TILE, W), lambda b, i: (b, i, blk))

    def halo(blk):
        if rev:
            return pl.BlockSpec((1, HALO, W), lambda b, i: (b, jnp.minimum((i + 1) * hb, S // HALO - 1), blk))
        return pl.BlockSpec((1, HALO, W), lambda b, i: (b, jnp.maximum(i * hb - 1, 0), blk))

    blks = (U_RW_K, U_RW_V, U_RW_R, 0)
    out = pl.BlockSpec((1, TOK_TILE, W), lambda b, i: (b, i, 0))
    return pl.pallas_call(
        functools.partial(_rw_prep_kernel, rev=rev),
        grid=(NB, n_t),
        in_specs=[tile(x) for x in blks] + [halo(x) for x in blks] + [
            pl.BlockSpec((1, 16, W), lambda b, i: (d, 0, 0)),
            pl.BlockSpec((1, 2 * LORA, W), lambda b, i: (d, 0, 0)),
            pl.BlockSpec((1, 2 * LORA, W), lambda b, i: (d, 0, 0)),
            pl.BlockSpec((W, W), lambda b, i: (0, 0)),
        ],
        out_specs=[out] * 7,
        out_shape=[jax.ShapeDtypeStruct((NB, S, W), F32)] * 7,
        compiler_params=_params(("arbitrary", "arbitrary"), 48),
        name="rw_prep",
    )(u3, u3, u3, ul3, u3, u3, u3, ul3, vec, w2e, a2e, ones_bd)


def _pair_rows(z, lo):
    return jnp.concatenate([jnp.where(lo, z, 0.0), jnp.where(lo, 0.0, z)], axis=0)


def _rw_scan_kernel(*refs):
    ins, y_refs, s_ref = (refs[0:6], refs[6:12]), refs[12:14], refs[14]

    @pl.when(pl.program_id(1) == 0)
    def _():
        s_ref[...] = jnp.zeros_like(s_ref)

    n = CHUNK
    t_i = lax.broadcasted_iota(jnp.int32, (n, n), 0)
    s_i = lax.broadcasted_iota(jnp.int32, (n, n), 1)
    t2 = lax.broadcasted_iota(jnp.int32, (2 * n, 2 * n), 0)
    s2 = lax.broadcasted_iota(jnp.int32, (2 * n, 2 * n), 1)
    same = (t2 >= n) == (s2 >= n)
    tt, ss = t2 & (n - 1), s2 & (n - 1)
    eye2 = t2 == s2
    lo = lax.broadcasted_iota(jnp.int32, (n, 128), 1) < HD
    incl1 = (s_i <= t_i, s_i >= t_i)
    incl2 = (same & (ss <= tt), same & (ss >= tt))
    strict2 = (same & (ss < tt), same & (ss > tt))
    last = (n - 1, 0)
    bf = lambda z: z.astype(BF)

    lw_all = [ins[d][0][0] for d in range(2)]
    cum_all = [_dot(incl1[d].astype(F32), lw_all[d], precision=HI) for d in range(2)]
    chains = [(d, p) for d in range(2) for p in range(HEADS // 2)]

    def operands(d, p):
        cs = slice(128 * p, 128 * p + 128)
        _, k_ref, v_ref, kk_ref, b_ref, r_ref = ins[d]
        lw, cum = lw_all[d][:, cs], cum_all[d][:, cs]
        tot = cum[last[d]:last[d] + 1, :]
        g, gp, gi, gc = jnp.exp(cum), jnp.exp(cum - lw), jnp.exp(-cum), jnp.exp(tot - cum)
        kk, b, k, r, v = kk_ref[0, :, cs], b_ref[0, :, cs], k_ref[0, :, cs], r_ref[0, :, cs], v_ref[0, :, cs]
        return dict(
            a2=bf(_pair_rows(-kk * gp, lo)), r2=_pair_rows(r * g, lo), b2=bf(_pair_rows(b * gi, lo)),
            k2=bf(_pair_rows(k * gi, lo)), bt2=bf(_pair_rows(b * gc, lo)), kt2=bf(_pair_rows(k * gc, lo)),
            v2=bf(_pair_rows(v, lo)), gtot=jnp.exp(tot))

    op = [operands(d, p) for d, p in chains]
    sc = [_dot_nt(jnp.concatenate([o["a2"], bf(o["r2"])], axis=0), jnp.concatenate([o["b2"], o["k2"]], axis=0))
          for o in op]
    m_ab = [jnp.where(strict2[d], s[:2 * n, :2 * n], 0.0) for (d, _), s in zip(chains, sc)]
    m_ak = [bf(jnp.where(strict2[d], s[:2 * n, 2 * n:], 0.0)) for (d, _), s in zip(chains, sc)]
    m_rb = [bf(jnp.where(incl2[d], s[2 * n:, :2 * n], 0.0)) for (d, _), s in zip(chains, sc)]
    m_rk = [bf(jnp.where(incl2[d], s[2 * n:, 2 * n:], 0.0)) for (d, _), s in zip(chains, sc)]
    x = [jnp.where(eye2, 1.0, 0.0) + m for m in m_ab]
    pw = m_ab
    for _ in range(int(np.log2(n)) - 1):
        pw = [_dot(bf(z), bf(z)) for z in pw]
        x = [xi + _dot(bf(xi), bf(z)) for xi, z in zip(x, pw)]
    xb = [bf(z) for z in x]
    mv = [bf(_dot(m, o["v2"])) for m, o in zip(m_ak, op)]
    wmb = [bf(_dot(z, o["a2"])) for z, o in zip(xb, op)]
    ub = [bf(_dot(z, m)) for z, m in zip(xb, mv)]
    rm = [o["r2"] + _dot(m, w) for o, m, w in zip(op, m_rb, wmb)]
    yu = [_dot(mb, u) + _dot(mk, o["v2"]) for mb, mk, u, o in zip(m_rb, m_rk, ub, op)]
    tm = [jnp.where(eye2, o["gtot"], 0.0) + _dot_tn(o["bt2"], w) for o, w in zip(op, wmb)]
    sv = [_dot_tn(o["bt2"], u) + _dot_tn(o["kt2"], o["v2"]) for o, u in zip(op, ub)]
    st = [bf(s_ref[d, p]) for d, p in chains]
    y2 = [_dot(bf(r), s) + y for r, s, y in zip(rm, st, yu)]
    s_new = [_dot(bf(t), s) + v for t, s, v in zip(tm, st, sv)]
    for (d, p), y, s in zip(chains, y2, s_new):
        y_refs[d][0, :, 128 * p:128 * p + 128] = y[:n] + y[n:]
        s_ref[d, p] = s


def _rw_scan(fwd, rev):
    def rev_idx(b, c):
        return (b, jnp.where(c < CTX_CHUNKS, CTX_CHUNKS - 1 - c, N_CHUNK + CTX_CHUNKS - 1 - c), 0)

    f_spec = pl.BlockSpec((1, CHUNK, W), lambda b, c: (b, c, 0))
    r_spec = pl.BlockSpec((1, CHUNK, W), rev_idx)
    return pl.pallas_call(
        _rw_scan_kernel,
        grid=(NB, N_CHUNK),
        in_specs=[f_spec] * 6 + [r_spec] * 6,
        out_specs=[f_spec, r_spec],
        out_shape=[jax.ShapeDtypeStruct((NB, S, W), F32)] * 2,
        scratch_shapes=[pltpu.VMEM((2, HEADS // 2, 128, 128), F32)],
        compiler_params=_params(("arbitrary", "arbitrary"), 32),
        name="rw_scan",
    )(*fwd, *rev)


def _rw_out_kernel(y0_ref, y1_ref, b0_ref, b1_ref, l_ref, g2_ref, avg_ref, ln_ref, o_ref):
    y = y0_ref[0] + y1_ref[0]
    avg = avg_ref[...]
    mu = _head_sum(y, avg)
    yc = y - mu
    var = _head_sum(yc * yc, avg)
    ln = ln_ref[...]
    y = yc * lax.rsqrt(var + GN_EPS) * ln[0:1] + ln[1:2] + b0_ref[0] + b1_ref[0]
    gate = _dot(jax.nn.sigmoid(l_ref[0].astype(F32)).astype(BF), g2_ref[...])
    o_ref[0] = (y * gate).astype(o_ref.dtype)


def _rw_out(y0, y1, bon0, bon1, ul3, g2e, avg_bd, ln):
    tile = pl.BlockSpec((1, TOK_TILE, W), lambda b, i: (b, i, 0))
    return pl.pallas_call(
        _rw_out_kernel,
        grid=(NB, S // TOK_TILE),
        in_specs=[tile] * 4 + [
            tile,
            pl.BlockSpec((W, W), lambda b, i: (0, 0)),
            pl.BlockSpec((W, W), lambda b, i: (0, 0)),
            pl.BlockSpec((8, W), lambda b, i: (0, 0)),
        ],
        out_specs=tile,
        out_shape=jax.ShapeDtypeStruct((NB, S, W), BF),
        compiler_params=_params(("arbitrary", "arbitrary"), 32),
        name="rw_out",
    )(y0, y1, bon0, bon1, ul3, g2e, avg_bd, ln)


def _dft_mats():
    gd = W // 4
    cc = np.arange(gd)
    ang = 2 * np.pi * ((cc[:, None] * cc[None, :]) % gd) / gd
    chan = np.zeros((W, 2 * W), np.float32)
    for g in range(4):
        chan[g * gd:(g + 1) * gd, g * gd:(g + 1) * gd] = np.cos(ang)
        chan[g * gd:(g + 1) * gd, W + g * gd:W + (g + 1) * gd] = np.sin(ang)
    cos = np.zeros((S, S), np.float32)
    nsin = np.zeros((S, S), np.float32)
    scale = np.zeros((S, 1), np.float32)
    for lo, n in ((0, CTX), (CTX, T)):
        tt = np.arange(n)
        a = 2 * np.pi * ((tt[:, None] * tt[None, :]) % n) / n
        cos[lo:lo + n, lo:lo + n] = np.cos(a)
        nsin[lo:lo + n, lo:lo + n] = -np.sin(a)
        scale[lo:lo + n] = (n * gd) ** -0.5
    return chan, cos, nsin, scale


def _fn_tok_kernel(cos_ref, nsin_ref, z_ref, sc_ref, o_ref):
    z = z_ref[0]
    f = _dot(cos_ref[...], z[:, 0:W]) + _dot(nsin_ref[...], z[:, W:2 * W])
    o_ref[0] = (f * sc_ref[...]).astype(o_ref.dtype)


def _fourier(u2, chan, cos, nsin, scale):
    zcs = _matmul(u2, chan, tm=1024, tn=2 * W, out_dtype=BF, a_col_blk=U_FN_X, k=W, name="fn_chan")
    return pl.pallas_call(
        _fn_tok_kernel,
        grid=(NB, S // TOK_TILE),
        in_specs=[
            pl.BlockSpec((TOK_TILE, S), lambda b, i: (i, 0)),
            pl.BlockSpec((TOK_TILE, S), lambda b, i: (i, 0)),
            pl.BlockSpec((1, S, 2 * W), lambda b, i: (b, 0, 0)),
            pl.BlockSpec((TOK_TILE, 1), lambda b, i: (i, 0)),
        ],
        out_specs=pl.BlockSpec((1, TOK_TILE, W), lambda b, i: (b, i, 0)),
        out_shape=jax.ShapeDtypeStruct((NB, S, W), BF),
        compiler_params=_params(("arbitrary", "arbitrary"), 40),
        name="fn_tok",
    )(cos, nsin, zcs.reshape(NB, S, 2 * W), scale)


def _sc_kernel(b_ref, c_ref, x_ref, cp_ref, xp_ref, cn_ref, xn_ref, w_ref, o_ref):
    i = pl.program_id(1)
    n_t = S // TOK_TILE
    f32 = lambda ref: ref[0].astype(F32)
    z = f32(c_ref) * f32(x_ref)
    zp = _shift_prev(z, (f32(cp_ref) * f32(xp_ref))[HALO - 1:HALO, :], i <= 1, False)
    zn = _shift_prev(z, (f32(cn_ref) * f32(xn_ref))[0:1, :], (i == 0) | (i == n_t - 1), True)
    w = w_ref[...]
    o_ref[0] = (f32(b_ref) * (zp * w[0:1] + z * w[1:2] + zn * w[2:3])).astype(o_ref.dtype)


def _short_conv(u3, conv_w):
    hb = TOK_TILE // HALO

    def tile(blk):
        return pl.BlockSpec((1, TOK_TILE, W), lambda b, i: (b, i, blk))

    def prev(blk):
        return pl.BlockSpec((1, HALO, W), lambda b, i: (b, jnp.maximum(i * hb - 1, 0), blk))

    def nxt(blk):
        return pl.BlockSpec((1, HALO, W), lambda b, i: (b, jnp.minimum((i + 1) * hb, S // HALO - 1), blk))

    return pl.pallas_call(
        _sc_kernel,
        grid=(NB, S // TOK_TILE),
        in_specs=[tile(U_SC_B), tile(U_SC_C), tile(U_SC_X), prev(U_SC_C), prev(U_SC_X), nxt(U_SC_C), nxt(U_SC_X),
                  pl.BlockSpec((8, W), lambda b, i: (0, 0))],
        out_specs=pl.BlockSpec((1, TOK_TILE, W), lambda b, i: (b, i, 0)),
        out_shape=jax.ShapeDtypeStruct((NB, S, W), BF),
        compiler_params=_params(("arbitrary", "arbitrary"), 32),
        name="short_conv",
    )(*([u3] * 7), conv_w)


def _merge_kernel(y0_ref, y1_ref, y2_ref, y3_ref, g0_ref, g1_ref, g2_ref, g3_ref, w_ref, o_ref):
    acc = None
    for n, (y_ref, g_ref) in enumerate(((y0_ref, g0_ref), (y1_ref, g1_ref), (y2_ref, g2_ref), (y3_ref, g3_ref))):
        gate = 0.5 * jnp.tanh(0.5 * g_ref[...].astype(F32)) + 0.5
        t = gate * _dot(y_ref[...], w_ref[n])
        acc = t if acc is None else acc + t
    o_ref[...] = acc.astype(o_ref.dtype)


def _merge(branches, u2, w_branch):
    tm, tn = 512, 512

    def gate(n):
        return pl.BlockSpec((tm, tn), lambda i, j: (i, U_GATES + n * (D // tn) + j))

    return pl.pallas_call(
        _merge_kernel,
        grid=(M // tm, D // tn),
        in_specs=[pl.BlockSpec((tm, W), lambda i, j: (i, 0))] * 4 + [gate(n) for n in range(4)] + [
            pl.BlockSpec((4, W, tn), lambda i, j: (0, 0, j))],
        out_specs=pl.BlockSpec((tm, tn), lambda i, j: (i, j)),
        out_shape=jax.ShapeDtypeStruct((M, D), BF),
        compiler_params=_params(("arbitrary", "arbitrary"), 32),
        name="merge",
    )(*branches, u2, u2, u2, u2, w_branch)


def _ffn_kernel(te_ref, nv_ref, x_ref, w1_ref, w3_ref, w2_ref, o_ref):
    i, f = pl.program_id(0), pl.program_id(1)

    @pl.when(f == 0)
    def _():
        o_ref[...] = jnp.zeros_like(o_ref)

    @pl.when(i < nv_ref[0])
    def _():
        x = x_ref[...]
        a = _dot(x, w1_ref[0].astype(BF))
        b = _dot(x, w3_ref[0].astype(BF))
        g = (a * jax.nn.sigmoid(a) * b).astype(BF)
        o_ref[...] += _dot(g, w2_ref[0].astype(BF))


def _ffn(x, w1, w3, w2, tile_expert, n_valid, *, tm, tf):
    rows = x.shape[0]
    n_tiles = rows // tm

    n_f = D_FF // tf

    def tile(i, nv):
        return jnp.minimum(i, nv[0] - 1)

    def ff(i, f, nv):
        return jnp.where(i < nv[0], f, n_f - 1)

    once = dict(pipeline_mode=pl.Buffered(1))
    return pl.pallas_call(
        _ffn_kernel,
        grid_spec=pltpu.PrefetchScalarGridSpec(
            num_scalar_prefetch=2,
            grid=(n_tiles, n_f),
            in_specs=[
                pl.BlockSpec((tm, D), lambda i, f, te, nv: (tile(i, nv), 0)),
                pl.BlockSpec((1, D, tf), lambda i, f, te, nv: (te[tile(i, nv)], 0, ff(i, f, nv))),
                pl.BlockSpec((1, D, tf), lambda i, f, te, nv: (te[tile(i, nv)], 0, ff(i, f, nv))),
                pl.BlockSpec((1, tf, D), lambda i, f, te, nv: (te[tile(i, nv)], ff(i, f, nv), 0)),
            ],
            out_specs=pl.BlockSpec((tm, D), lambda i, f, te, nv: (i, 0), **once),
        ),
        out_shape=jax.ShapeDtypeStruct((rows, D), F32),
        compiler_params=_params(("arbitrary", "arbitrary"), 56),
        name="ffn",
    )(tile_expert, n_valid, x, w1, w3, w2)


def _gather_kernel(idx_ref, live_ref, src_ref, *rest, n_slots, n_rows, weighted):
    wt_ref = rest[0] if weighted else None
    o_ref, buf, sem = rest[-3:]
    g, n_live = pl.program_id(0), live_ref[0]

    def issue(step, slot):
        def body(k, carry):
            for s in range(n_slots):
                row = idx_ref[s * n_rows + step * GATHER_ROWS + k]
                pltpu.make_async_copy(src_ref.at[pl.ds(row, 1)], buf.at[slot, pl.ds(s * GATHER_ROWS + k, 1)],
                                      sem.at[slot]).start()
            return carry
        lax.fori_loop(0, GATHER_ROWS, body, 0, unroll=8)

    @pl.when(g == 0)
    def _():
        issue(0, 0)

    @pl.when(g + 1 < n_live)
    def _():
        issue(g + 1, (g + 1) % 2)

    @pl.when(g < n_live)
    def _():
        slot = g % 2
        pltpu.make_async_copy(src_ref.at[pl.ds(0, n_slots * GATHER_ROWS)], buf.at[slot], sem.at[slot]).wait()
        acc = None
        for s in range(n_slots):
            rows = buf[slot, s * GATHER_ROWS:(s + 1) * GATHER_ROWS]
            if wt_ref is not None:
                rows = rows * wt_ref[:, s:s + 1]
            acc = rows if acc is None else acc + rows
        o_ref[...] = acc.astype(o_ref.dtype)

    @pl.when(g >= n_live)
    def _():
        o_ref[...] = jnp.zeros_like(o_ref)


def _row_gather(src, idx, n_slots, out_dtype, n_live_rows=None, weights=None):
    n_rows = idx.shape[0] // n_slots
    n_steps = n_rows // GATHER_ROWS
    live = jnp.full((1,), n_steps, jnp.int32) if n_live_rows is None else n_live_rows // GATHER_ROWS
    weighted = weights is not None
    wt_spec = [pl.BlockSpec((GATHER_ROWS, n_slots), lambda g, idx, live: (g, 0))] if weighted else []
    return pl.pallas_call(
        functools.partial(_gather_kernel, n_slots=n_slots, n_rows=n_rows, weighted=weighted),
        grid_spec=pltpu.PrefetchScalarGridSpec(
            num_scalar_prefetch=2,
            grid=(n_steps,),
            in_specs=[pl.BlockSpec(memory_space=pl.ANY)] + wt_spec,
            out_specs=pl.BlockSpec((GATHER_ROWS, D), lambda g, idx, live: (g, 0)),
            scratch_shapes=[pltpu.VMEM((2, n_slots * GATHER_ROWS, D), F32), pltpu.SemaphoreType.DMA((2,))],
        ),
        out_shape=jax.ShapeDtypeStruct((n_rows, D), out_dtype),
        compiler_params=_params(("arbitrary",), 32),
        name="row_gather",
    )(idx, live, src, *([weights] if weighted else []))


def _moe(hf, logits, tok, w1, w3, w2, layer):
    n_tok = tok.shape[0]
    n_tiles = (2 * n_tok) // MOE_TM + N_EXP
    top_val, top_idx = lax.top_k(logits, 2)
    top_p = jax.nn.softmax(top_val, axis=-1)
    e_flat = top_idx.reshape(-1)
    onehot = (e_flat[:, None] == jnp.arange(N_EXP)[None, :]).astype(jnp.int32)
    rank = jnp.take_along_axis(jnp.cumsum(onehot, axis=0) - onehot, e_flat[:, None], axis=1)[:, 0]
    count = jnp.sum(onehot, axis=0)
    tiles_per = (count + MOE_TM - 1) // MOE_TM
    tile_end = jnp.cumsum(tiles_per)
    group_start = (tile_end - tiles_per) * MOE_TM
    pos = group_start[e_flat] + rank
    n_valid = tile_end[-1:].astype(jnp.int32)
    tile_expert = jnp.minimum(jnp.searchsorted(tile_end, jnp.arange(n_tiles), side="right"),
                              N_EXP - 1).astype(jnp.int32)
    rows = n_tiles * MOE_TM
    src = jnp.zeros((rows,), jnp.int32).at[pos].set(jnp.repeat(tok, 2))
    xs = _row_gather(hf, src, 1, BF, n_live_rows=n_valid * MOE_TM)
    ys = _ffn(xs, w1, w3, w2, tile_expert + layer * N_EXP, n_valid, tm=MOE_TM, tf=FF_TILE)
    return _row_gather(ys, pos.reshape(n_tok, 2).T.reshape(-1), 2, F32, weights=top_p)


def _low_rank_w_in(w, layer):
    return jnp.concatenate([w[layer, :, 2048:2304], w[layer, :, 3328:3456], jnp.zeros((D, 128), w.dtype)], axis=1)


def _rows_at(mat, off, n=W):
    return jnp.zeros((n, W), F32).at[off:off + mat.shape[0]].set(mat)


def _pad_rows(rows, n):
    z = jnp.stack(rows)
    return jnp.concatenate([z, jnp.zeros((n - z.shape[0],) + z.shape[1:], z.dtype)], axis=0)


def kernel(x, c, ctx, c_ctx, ada_w, ada_b, norm_mix, norm_ffn, w_in, na_rel_bias, rw_mu_kvr, rw_mu_lora, rw_w0, rw_w2, rw_a0, rw_a2, rw_k_k, rw_k_a, rw_r_k, rw_g2, rw_ln_w, rw_ln_b, sc_conv, w_branch, w_out, ffn_w1, ffn_w3, ffn_w2, moe_router, moe_w1, moe_w3, moe_w2, norm_final):
    xs = jnp.concatenate([ctx, x], axis=1)
    cvec = jnp.concatenate([c, c_ctx[None], jnp.zeros((3, D), F32)], axis=0)
    mods = _ada(cvec, ada_w, ada_b).reshape(DEPTH, 8, 6, D)

    def mod_table(gate, scale, shift):
        def pick(sel):
            if sel is None:
                return jnp.zeros((NB, 2, D), F32)
            v = mods[sel[0], :, sel[1]]
            return jnp.stack([jnp.broadcast_to(v[4], (NB, D)), v[:NB]], axis=1)
        t = jnp.stack([pick(gate), pick(scale), pick(shift)], axis=2)
        return jnp.concatenate([t, jnp.zeros((NB, 2, 5, D), F32)], axis=2)

    head_id = np.arange(W) // HD
    ones_bd = jnp.asarray((head_id[:, None] == head_id[None, :]).astype(np.float32), BF)
    chan, cos, nsin, fscale = _dft_mats()
    chan, cos, nsin, fscale = jnp.asarray(chan, BF), jnp.asarray(cos, BF), jnp.asarray(nsin, BF), jnp.asarray(fscale)

    delta = None
    for l in range(DEPTH):
        mod = mod_table(None if l == 0 else (l - 1, 5), (l, 1), (l, 0))
        if l == 0:
            (h,) = _resid_norm(xs, None, mod, norm_mix[l])
        else:
            xs, h = _resid_norm(xs, delta, mod, norm_mix[l])
        u2 = _w_in(h.reshape(M, D), w_in, l)
        u3 = u2.reshape(NB, S, U_COLS)
        ul3 = _matmul(h.reshape(M, D), _low_rank_w_in(w_in, l), tm=1024, tn=W, out_dtype=BF,
                      name="w_in_low_rank").reshape(NB, S, W)

        y_na = _attention(u3, _na_bias_table(na_rel_bias[l]))

        lora_mu = jnp.zeros((2, W), F32)
        lora_mu = lora_mu.at[:, LORA_W:LORA_W + 2 * LORA].set(
            jnp.broadcast_to(rw_mu_lora[l][:, 0].reshape(1, 2 * LORA), (2, 2 * LORA)))
        lora_mu = lora_mu.at[:, LORA_A:LORA_A + 2 * LORA].set(
            jnp.broadcast_to(rw_mu_lora[l][:, 1].reshape(1, 2 * LORA), (2, 2 * LORA)))
        vec = jnp.stack([_pad_rows([rw_mu_kvr[l][d, 0], rw_mu_kvr[l][d, 1], rw_mu_kvr[l][d, 2], lora_mu[d],
                                    rw_w0[l][d], rw_a0[l][d], rw_k_k[l], rw_k_a[l], rw_r_k[l]], 16)
                         for d in range(2)])
        w2e = jnp.stack([_rows_at(rw_w2[l][d], LORA * d, 2 * LORA) for d in range(2)])
        a2e = jnp.stack([_rows_at(rw_a2[l][d], LORA * d, 2 * LORA) for d in range(2)])
        seqs = [_rw_prep(u3, ul3, vec, w2e, a2e, ones_bd, d) for d in range(2)]
        ys = _rw_scan(seqs[0][:6], seqs[1][:6])
        bons = [seqs[0][6], seqs[1][6]]
        y_rw = _rw_out(ys[0], ys[1], bons[0], bons[1], ul3, _rows_at(rw_g2[l], LORA_G).astype(BF),
                       ones_bd / HD, _pad_rows([rw_ln_w[l], rw_ln_b[l]], 8))

        y_fn = _fourier(u2, chan, cos, nsin, fscale)
        y_sc = _short_conv(u3, _pad_rows([sc_conv[l][0], sc_conv[l][1], sc_conv[l][2]], 8))

        merged = _merge([z.reshape(M, W) for z in (y_na, y_rw, y_fn, y_sc)], u2, w_branch[l].astype(BF))
        merged = merged.reshape(NB, S, D)
        w_o = w_out[l].astype(BF)

        mod = mod_table((l, 2), (l, 4), (l, 3))
        i = l // 2
        last = l == DEPTH - 1
        if l % 2 == 0:
            xs, hf = _resid_norm(xs, merged, mod, norm_ffn[l], proj=w_o)
            delta = _ffn(hf.reshape(M, D), ffn_w1, ffn_w3, ffn_w2,
                         jnp.full((M // 1024,), i, jnp.int32), jnp.full((1,), M // 1024, jnp.int32),
                         tm=1024, tf=FF_TILE).reshape(NB, S, D)
            delta = delta[:, CTX:] if last else delta
        else:
            router = jnp.concatenate([moe_router[i], jnp.zeros((D, 128 - N_EXP), F32)], axis=1)
            xs, hf, logits = _resid_norm(xs, merged, mod, norm_ffn[l], router=router, proj=w_o)
            n_moe = moe_w1.shape[0]
            first = CTX if last else 0
            tok = jnp.arange(M, dtype=jnp.int32).reshape(NB, S)[:, first:].reshape(-1)
            delta = _moe(hf.reshape(M, D), logits[:, first:, :N_EXP].reshape(-1, N_EXP), tok,
                         moe_w1.reshape(n_moe * N_EXP, D, D_FF),
                         moe_w3.reshape(n_moe * N_EXP, D, D_FF), moe_w2.reshape(n_moe * N_EXP, D_FF, D), i)
            delta = delta.reshape(NB, -1, D)

    mod = mod_table((DEPTH - 1, 5), None, None)
    (out,) = _resid_norm(xs, delta, mod, norm_final, final=True)
    return out
```

```python
import functools

import jax
import jax.numpy as jnp
import numpy as np
from jax import lax
from jax.experimental import pallas as pl
from jax.experimental.pallas import tpu as pltpu

BF = jnp.bfloat16
F32 = jnp.float32
HI = lax.Precision.HIGHEST

D = 2048
NB = 4
T = 2048
CTX = 256
S = CTX + T
M = NB * S
DEPTH = 4
GRID_W = 64
ROWS = T // GRID_W
WIN_H = 8
WIN_W = 16
HEADS = 8
HD = 64
W = HEADS * HD
LORA = 64
D_FF = 5632
N_EXP = 8
RMS_EPS = 1e-6
GN_EPS = 64e-5
NEG = -1e30

U_NA_K, U_NA_V, U_RW_K, U_RW_V, U_NA_Q, U_RW_R, U_FN_X, U_SC_B, U_SC_C, U_SC_X, U_GATES = range(11)
W_IN_TILE = 1024
W_IN_SRC = (0, 1024, 2304, 3456, 4480) + tuple(5504 + W_IN_TILE * t for t in range(8))
U_COLS = W_IN_TILE * len(W_IN_SRC)
LORA_W, LORA_A, LORA_G = 0, 128, 256

CHUNK = 64
N_CHUNK = S // CHUNK
CTX_CHUNKS = CTX // CHUNK
TOK_TILE = 256
MOE_TM = 1024
GATHER_ROWS = 256
FF_TILE = 512
HALO = 16


def _params(sem, vmem_mb):
    return pltpu.CompilerParams(dimension_semantics=sem, vmem_limit_bytes=vmem_mb << 20)


def _dot(a, b, **kw):
    return jnp.dot(a, b, preferred_element_type=F32, **kw)


def _dot_nt(a, b):
    return lax.dot_general(a, b, (((1,), (1,)), ((), ())), preferred_element_type=F32)


def _dot_tn(a, b):
    return lax.dot_general(a, b, (((0,), (0,)), ((), ())), preferred_element_type=F32)


def _head_sum(z, sel):
    hi = z.astype(BF)
    lo = (z - hi.astype(F32)).astype(BF)
    return _dot(hi, sel) + _dot(lo, sel)


def _ada_kernel(c_ref, w_ref, b_ref, o_ref):
    c = c_ref[...]
    a = (c * jax.nn.sigmoid(c)).astype(BF)
    o_ref[0] = _dot(a, w_ref[0].astype(BF)) + b_ref[0]


def _ada(cvec, ada_w, ada_b):
    tn = 1024
    return pl.pallas_call(
        _ada_kernel,
        grid=(DEPTH, 6 * D // tn),
        in_specs=[
            pl.BlockSpec((8, D), lambda l, j: (0, 0)),
            pl.BlockSpec((1, D, tn), lambda l, j: (l, 0, j)),
            pl.BlockSpec((1, 1, tn), lambda l, j: (l, 0, j)),
        ],
        out_specs=pl.BlockSpec((1, 8, tn), lambda l, j: (l, 0, j)),
        out_shape=jax.ShapeDtypeStruct((DEPTH, 8, 6 * D), F32),
        compiler_params=_params(("arbitrary", "arbitrary"), 40),
        name="ada",
    )(cvec, ada_w, ada_b.reshape(DEPTH, 1, 6 * D))


def _rn_kernel(*refs, has_delta, has_proj, has_router, final):
    it = iter(refs)
    x_ref = next(it)
    d_ref = next(it) if has_delta else None
    p_ref = next(it) if has_proj else None
    mod_ref = next(it)
    g_ref = next(it)
    r_ref = next(it) if has_router else None
    xo_ref = next(it) if (has_delta and not final) else None
    h_ref = next(it)
    lg_ref = next(it) if has_router else None
    x = x_ref[0]
    mod = mod_ref[0, 0]
    if has_delta:
        d = _dot(d_ref[0], p_ref[...]) if has_proj else d_ref[0]
        x = x + mod[0:1] * d
        if xo_ref is not None:
            xo_ref[0] = x
    y = x * lax.rsqrt(jnp.mean(x * x, axis=-1, keepdims=True) + RMS_EPS) * g_ref[...]
    if final:
        h_ref[0] = y
        return
    h = y * (1.0 + mod[1:2]) + mod[2:3]
    h_ref[0] = h.astype(h_ref.dtype)
    if has_router:
        h_hi = h.astype(BF)
        h_lo = (h - h_hi.astype(F32)).astype(BF)
        lg_ref[0] = _dot(h_hi, r_ref[0]) + (_dot(h_lo, r_ref[0]) + _dot(h_hi, r_ref[1]))


def _resid_norm(x, delta, mod, g, router=None, final=False, proj=None):
    has_delta = delta is not None
    has_router = router is not None
    has_proj = proj is not None
    n_t = S // TOK_TILE
    off = CTX // TOK_TILE if final else 0
    grid = (NB, n_t - off)
    tok = pl.BlockSpec((1, TOK_TILE, D), lambda b, i: (b, i + off, 0))
    in_specs = [tok] + ([pl.BlockSpec((1, TOK_TILE, D), lambda b, i: (b, i, 0))] if has_delta else [])
    in_specs += [pl.BlockSpec((D, D), lambda b, i: (0, 0), pipeline_mode=pl.Buffered(1))] if has_proj else []
    in_specs += [
        pl.BlockSpec((1, 1, 8, D), lambda b, i: (b, jnp.minimum(i + off, 1), 0, 0)),
        pl.BlockSpec((1, D), lambda b, i: (0, 0)),
    ]
    args = [x] + ([delta] if has_delta else []) + ([proj] if has_proj else []) + [mod, g.reshape(1, D)]
    out_specs, out_shape = [], []
    if final:
        out_specs.append(pl.BlockSpec((1, TOK_TILE, D), lambda b, i: (b, i, 0)))
        out_shape.append(jax.ShapeDtypeStruct((NB, T, D), F32))
    else:
        if has_delta:
            out_specs.append(tok)
            out_shape.append(jax.ShapeDtypeStruct((NB, S, D), F32))
        out_specs.append(tok)
        out_shape.append(jax.ShapeDtypeStruct((NB, S, D), F32 if has_router else BF))
    if has_router:
        in_specs.append(pl.BlockSpec((2, D, 128), lambda b, i: (0, 0, 0)))
        r_hi = router.astype(BF)
        args.append(jnp.stack([r_hi, (router - r_hi.astype(F32)).astype(BF)]))
        out_specs.append(pl.BlockSpec((1, TOK_TILE, 128), lambda b, i: (b, i, 0)))
        out_shape.append(jax.ShapeDtypeStruct((NB, S, 128), F32))
    return pl.pallas_call(
        functools.partial(_rn_kernel, has_delta=has_delta, has_proj=has_proj, has_router=has_router, final=final),
        grid=grid,
        in_specs=in_specs,
        out_specs=out_specs,
        out_shape=out_shape,
        compiler_params=_params(("arbitrary", "arbitrary"), 40),
        name="resid_norm",
    )(*args)


def _mm_kernel(a_ref, w_ref, o_ref):
    o_ref[...] = _dot(a_ref[...].astype(BF), w_ref[...].astype(BF)).astype(o_ref.dtype)


def _matmul(a, w, *, tm, tn, out_dtype, a_col_blk=0, k=None, name="matmul"):
    m = a.shape[0]
    k = k or a.shape[1]
    n = w.shape[1]
    return pl.pallas_call(
        _mm_kernel,
        grid=(n // tn, m // tm),
        in_specs=[
            pl.BlockSpec((tm, k), lambda j, i: (i, a_col_blk)),
            pl.BlockSpec((k, tn), lambda j, i: (0, j)),
        ],
        out_specs=pl.BlockSpec((tm, tn), lambda j, i: (i, j)),
        out_shape=jax.ShapeDtypeStruct((m, n), out_dtype),
        compiler_params=_params(("arbitrary", "arbitrary"), 48),
        name=name,
    )(a, w)


def _w_in_kernel(off_ref, a_ref, w_ref, o_ref, wb_ref):
    @pl.when(pl.program_id(1) == 0)
    def _():
        wb_ref[...] = w_ref[0].astype(BF)

    o_ref[...] = _dot(a_ref[...], wb_ref[...]).astype(o_ref.dtype)


def _w_in(h, w, layer):
    tm = 1024
    return pl.pallas_call(
        _w_in_kernel,
        grid_spec=pltpu.PrefetchScalarGridSpec(
            num_scalar_prefetch=1,
            grid=(len(W_IN_SRC), M // tm),
            in_specs=[
                pl.BlockSpec((tm, D), lambda j, i, off: (i, 0)),
                pl.BlockSpec((pl.Element(1), pl.Element(D), pl.Element(W_IN_TILE)),
                             lambda j, i, off: (layer, 0, pl.multiple_of(off[j], 128))),
            ],
            out_specs=pl.BlockSpec((tm, W_IN_TILE), lambda j, i, off: (i, j)),
            scratch_shapes=[pltpu.VMEM((D, W_IN_TILE), BF)],
        ),
        out_shape=jax.ShapeDtypeStruct((M, U_COLS), BF),
        compiler_params=_params(("arbitrary", "arbitrary"), 48),
        name="w_in",
    )(jnp.asarray(W_IN_SRC, jnp.int32), h, w)


def _na_window(r):
    rr = r - CTX // GRID_W
    rs = jnp.clip(rr - WIN_H // 2, 0, ROWS - WIN_H)
    return rr, rs


def _na_kernel(q_ref, k_ref, v_ref, bias_ref, o_ref):
    _, rs = _na_window(pl.program_id(1))
    kstart = pl.multiple_of(CTX + GRID_W * rs, GRID_W)
    lo = lax.broadcasted_iota(jnp.int32, (GRID_W, 128), 1) < HD
    n_nb = WIN_H * GRID_W
    pairs = [slice(128 * p, 128 * p + 128) for p in range(HEADS // 2)]
    rowmax = lambda z: jnp.max(z, axis=-1, keepdims=True)
    rowsum = lambda z: jnp.sum(z, axis=-1, keepdims=True)
    q = [q_ref[0, :, cs].astype(F32) * (HD ** -0.5) for cs in pairs]
    q2 = [jnp.concatenate([jnp.where(lo, z, 0.0), jnp.where(lo, 0.0, z)], axis=0).astype(BF) for z in q]
    s_nb = [_dot_nt(z, k_ref[0, pl.ds(kstart, n_nb), cs]) + bias_ref[0, 128 * p:128 * p + 128, :]
            for p, (z, cs) in enumerate(zip(q2, pairs))]
    s_cx = [_dot_nt(z, k_ref[0, 0:CTX, cs]) for z, cs in zip(q2, pairs)]
    m = [jnp.maximum(rowmax(a), rowmax(b)) for a, b in zip(s_nb, s_cx)]
    e_nb = [jnp.exp(a - mi) for a, mi in zip(s_nb, m)]
    e_cx = [jnp.exp(b - mi) for b, mi in zip(s_cx, m)]
    l = [rowsum(a) + rowsum(b) for a, b in zip(e_nb, e_cx)]
    o2 = [(_dot(a.astype(BF), v_ref[0, pl.ds(kstart, n_nb), cs]) + _dot(b.astype(BF), v_ref[0, 0:CTX, cs])) / li
          for a, b, li, cs in zip(e_nb, e_cx, l, pairs)]
    for z, cs in zip(o2, pairs):
        o_ref[0, :, cs] = jnp.where(lo, z[0:GRID_W], z[GRID_W:]).astype(o_ref.dtype)


def _na_bias_table(rel_bias):
    cols = np.arange(GRID_W)
    col_start = np.clip(cols - WIN_W // 2, 0, GRID_W - WIN_W)
    col_ok = (cols[None, :] >= col_start[:, None]) & (cols[None, :] < col_start[:, None] + WIN_W)
    dc = np.clip(cols[None, :] - cols[:, None] + (WIN_W - 1), 0, 2 * WIN_W - 2)
    pick = (dc[None] == np.arange(2 * WIN_W - 1)[:, None, None]).astype(np.float32)
    by_dr = jnp.einsum("hac,cqk->haqk", rel_bias.astype(F32), pick, precision=HI)
    by_dr = jnp.where(col_ok[None, None], by_dr, NEG)
    tab = jnp.stack([by_dr[:, d0:d0 + WIN_H] for d0 in range(WIN_H)])
    tab = jnp.transpose(tab, (0, 1, 3, 2, 4)).reshape(WIN_H, HEADS * GRID_W, WIN_H * GRID_W)
    return jnp.concatenate([tab, jnp.full((1,) + tab.shape[1:], NEG, F32)], axis=0)


def _attention(u3, bias_tab):
    def bias_idx(b, r):
        rr, rs = _na_window(r)
        return (jnp.where(rr < 0, WIN_H, rs - rr + WIN_H - 1), 0, 0)

    return pl.pallas_call(
        _na_kernel,
        grid=(NB, S // GRID_W),
        in_specs=[
            pl.BlockSpec((1, GRID_W, W), lambda b, r: (b, r, U_NA_Q)),
            pl.BlockSpec((1, S, W), lambda b, r: (b, 0, U_NA_K)),
            pl.BlockSpec((1, S, W), lambda b, r: (b, 0, U_NA_V)),
            pl.BlockSpec((1, HEADS * GRID_W, WIN_H * GRID_W), bias_idx),
        ],
        out_specs=pl.BlockSpec((1, GRID_W, W), lambda b, r: (b, r, 0)),
        out_shape=jax.ShapeDtypeStruct((NB, S, W), BF),
        compiler_params=_params(("arbitrary", "arbitrary"), 48),
        name="attention",
    )(u3, u3, u3, bias_tab)


def _shift_prev(z, halo_row, first_is_start, rev):
    n = z.shape[0]
    row = lax.broadcasted_iota(jnp.int32, z.shape, 0)
    edge = jnp.where(first_is_start, 0.0, halo_row)
    if rev:
        return jnp.where(row == n - 1, edge, pltpu.roll(z, n - 1, 0))
    return jnp.where(row == 0, edge, pltpu.roll(z, 1, 0))


def _rw_prep_kernel(k_ref, v_ref, r_ref, l_ref, kh_ref, vh_ref, rh_ref, lh_ref, vec_ref, w2_ref, a2_ref,
                    ones_ref, lw_o, k_o, v_o, kk_o, b_o, r_o, bonus_o, *, rev):
    i = pl.program_id(1)
    n_t = S // TOK_TILE
    start = (i == n_t - 1) | (i == 0) if rev else (i <= 1)
    hrow = 0 if rev else HALO - 1
    vec = vec_ref[0]
    mu_k, mu_v, mu_r, mu_l, w0, a0, k_k, k_a, r_k = (vec[j:j + 1] for j in range(9))

    def lerp(ref, href, mu):
        z = ref[0].astype(F32)
        prev = _shift_prev(z, href[0].astype(F32)[hrow:hrow + 1, :], start, rev)
        return z + (prev - z) * mu

    lz = lerp(l_ref, lh_ref, mu_l)
    wz = w0 + _dot(jnp.tanh(lz[:, LORA_W:LORA_W + 128]), w2_ref[0], precision=HI)
    a = jax.nn.sigmoid(a0 + _dot(lz[:, LORA_A:LORA_A + 128], a2_ref[0], precision=HI))
    softplus = jnp.maximum(-wz, 0.0) + jnp.log1p(jnp.exp(-jnp.abs(wz)))
    lw_o[0] = -jnp.exp(-softplus - 0.5)
    k1 = lerp(k_ref, kh_ref, mu_k)
    kkr = k1 * k_k
    ones = ones_ref[...]
    nrm = jnp.sqrt(_head_sum(kkr * kkr, ones))
    kk = kkr / jnp.maximum(nrm, 1e-12)
    k2 = k1 * (1.0 + (a - 1.0) * k_a)
    v1 = lerp(v_ref, vh_ref, mu_v)
    r1 = lerp(r_ref, rh_ref, mu_r)
    k_o[0] = k2
    v_o[0] = v1
    kk_o[0] = kk
    b_o[0] = kk * a
    r_o[0] = r1
    bonus_o[0] = _head_sum(r1 * k2 * r_k, ones) * v1


def _rw_prep(u3, ul3, vec, w2e, a2e, ones_bd, d):
    rev = d == 1
    n_t = S // TOK_TILE
    hb = TOK_TILE // HALO

    def tile(blk):
        return pl.BlockSpec((1, TOK_TILE, W), lambda b, i: (b, i, blk))

    def halo(blk):
        if rev:
            return pl.BlockSpec((1, HALO, W), lambda b, i: (b, jnp.minimum((i + 1) * hb, S // HALO - 1), blk))
        return pl.BlockSpec((1, HALO, W), lambda b, i: (b, jnp.maximum(i * hb - 1, 0), blk))

    blks = (U_RW_K, U_RW_V, U_RW_R, 0)
    out = pl.BlockSpec((1, TOK_TILE, W), lambda b, i: (b, i, 0))
    return pl.pallas_call(
        functools.partial(_rw_prep_kernel, rev=rev),
        grid=(NB, n_t),
        in_specs=[tile(x) for x in blks] + [halo(x) for x in blks] + [
            pl.BlockSpec((1, 16, W), lambda b, i: (d, 0, 0)),
            pl.BlockSpec((1, 2 * LORA, W), lambda b, i: (d, 0, 0)),
            pl.BlockSpec((1, 2 * LORA, W), lambda b, i: (d, 0, 0)),
            pl.BlockSpec((W, W), lambda b, i: (0, 0)),
        ],
        out_specs=[out] * 7,
        out_shape=[jax.ShapeDtypeStruct((NB, S, W), F32)] * 7,
        compiler_params=_params(("arbitrary", "arbitrary"), 48),
        name="rw_prep",
    )(u3, u3, u3, ul3, u3, u3, u3, ul3, vec, w2e, a2e, ones_bd)


def _pair_rows(z, lo):
    return jnp.concatenate([jnp.where(lo, z, 0.0), jnp.where(lo, 0.0, z)], axis=0)


def _rw_scan_kernel(*refs):
    ins, y_refs, s_ref = (refs[0:6], refs[6:12]), refs[12:14], refs[14]

    @pl.when(pl.program_id(1) == 0)
    def _():
        s_ref[...] = jnp.zeros_like(s_ref)

    n = CHUNK
    t_i = lax.broadcasted_iota(jnp.int32, (n, n), 0)
    s_i = lax.broadcasted_iota(jnp.int32, (n, n), 1)
    t2 = lax.broadcasted_iota(jnp.int32, (2 * n, 2 * n), 0)
    s2 = lax.broadcasted_iota(jnp.int32, (2 * n, 2 * n), 1)
    same = (t2 >= n) == (s2 >= n)
    tt, ss = t2 & (n - 1), s2 & (n - 1)
    eye2 = t2 == s2
    lo = lax.broadcasted_iota(jnp.int32, (n, 128), 1) < HD
    incl1 = (s_i <= t_i, s_i >= t_i)
    incl2 = (same & (ss <= tt), same & (ss >= tt))
    strict2 = (same & (ss < tt), same & (ss > tt))
    last = (n - 1, 0)
    bf = lambda z: z.astype(BF)

    lw_all = [ins[d][0][0] for d in range(2)]
    cum_all = [_dot(incl1[d].astype(F32), lw_all[d], precision=HI) for d in range(2)]
    chains = [(d, p) for d in range(2) for p in range(HEADS // 2)]

    def operands(d, p):
        cs = slice(128 * p, 128 * p + 128)
        _, k_ref, v_ref, kk_ref, b_ref, r_ref = ins[d]
        lw, cum = lw_all[d][:, cs], cum_all[d][:, cs]
        tot = cum[last[d]:last[d] + 1, :]
        g, gp, gi, gc = jnp.exp(cum), jnp.exp(cum - lw), jnp.exp(-cum), jnp.exp(tot - cum)
        kk, b, k, r, v = kk_ref[0, :, cs], b_ref[0, :, cs], k_ref[0, :, cs], r_ref[0, :, cs], v_ref[0, :, cs]
        return dict(
            a2=bf(_pair_rows(-kk * gp, lo)), r2=_pair_rows(r * g, lo), b2=bf(_pair_rows(b * gi, lo)),
            k2=bf(_pair_rows(k * gi, lo)), bt2=bf(_pair_rows(b * gc, lo)), kt2=bf(_pair_rows(k * gc, lo)),
            v2=bf(_pair_rows(v, lo)), gtot=jnp.exp(tot))

    op = [operands(d, p) for d, p in chains]
    sc = [_dot_nt(jnp.concatenate([o["a2"], bf(o["r2"])], axis=0), jnp.concatenate([o["b2"], o["k2"]], axis=0))
          for o in op]
    m_ab = [jnp.where(strict2[d], s[:2 * n, :2 * n], 0.0) for (d, _), s in zip(chains, sc)]
    m_ak = [bf(jnp.where(strict2[d], s[:2 * n, 2 * n:], 0.0)) for (d, _), s in zip(chains, sc)]
    m_rb = [bf(jnp.where(incl2[d], s[2 * n:, :2 * n], 0.0)) for (d, _), s in zip(chains, sc)]
    m_rk = [bf(jnp.where(incl2[d], s[2 * n:, 2 * n:], 0.0)) for (d, _), s in zip(chains, sc)]
    x = [jnp.where(eye2, 1.0, 0.0) + m for m in m_ab]
    pw = m_ab
    for _ in range(int(np.log2(n)) - 1):
        pw = [_dot(bf(z), bf(z)) for z in pw]
        x = [xi + _dot(bf(xi), bf(z)) for xi, z in zip(x, pw)]
    xb = [bf(z) for z in x]
    mv = [bf(_dot(m, o["v2"])) for m, o in zip(m_ak, op)]
    wmb = [bf(_dot(z, o["a2"])) for z, o in zip(xb, op)]
    ub = [bf(_dot(z, m)) for z, m in zip(xb, mv)]
    rm = [o["r2"] + _dot(m, w) for o, m, w in zip(op, m_rb, wmb)]
    yu = [_dot(mb, u) + _dot(mk, o["v2"]) for mb, mk, u, o in zip(m_rb, m_rk, ub, op)]
    tm = [jnp.where(eye2, o["gtot"], 0.0) + _dot_tn(o["bt2"], w) for o, w in zip(op, wmb)]
    sv = [_dot_tn(o["bt2"], u) + _dot_tn(o["kt2"], o["v2"]) for o, u in zip(op, ub)]
    st = [bf(s_ref[d, p]) for d, p in chains]
    y2 = [_dot(bf(r), s) + y for r, s, y in zip(rm, st, yu)]
    s_new = [_dot(bf(t), s) + v for t, s, v in zip(tm, st, sv)]
    for (d, p), y, s in zip(chains, y2, s_new):
        y_refs[d][0, :, 128 * p:128 * p + 128] = y[:n] + y[n:]
        s_ref[d, p] = s


def _rw_scan(fwd, rev):
    def rev_idx(b, c):
        return (b, jnp.where(c < CTX_CHUNKS, CTX_CHUNKS - 1 - c, N_CHUNK + CTX_CHUNKS - 1 - c), 0)

    f_spec = pl.BlockSpec((1, CHUNK, W), lambda b, c: (b, c, 0))
    r_spec = pl.BlockSpec((1, CHUNK, W), rev_idx)
    return pl.pallas_call(
        _rw_scan_kernel,
        grid=(NB, N_CHUNK),
        in_specs=[f_spec] * 6 + [r_spec] * 6,
        out_specs=[f_spec, r_spec],
        out_shape=[jax.ShapeDtypeStruct((NB, S, W), F32)] * 2,
        scratch_shapes=[pltpu.VMEM((2, HEADS // 2, 128, 128), F32)],
        compiler_params=_params(("arbitrary", "arbitrary"), 32),
        name="rw_scan",
    )(*fwd, *rev)


def _rw_out_kernel(y0_ref, y1_ref, b0_ref, b1_ref, l_ref, g2_ref, avg_ref, ln_ref, o_ref):
    y = y0_ref[0] + y1_ref[0]
    avg = avg_ref[...]
    mu = _head_sum(y, avg)
    yc = y - mu
    var = _head_sum(yc * yc, avg)
    ln = ln_ref[...]
    y = yc * lax.rsqrt(var + GN_EPS) * ln[0:1] + ln[1:2] + b0_ref[0] + b1_ref[0]
    gate = _dot(jax.nn.sigmoid(l_ref[0].astype(F32)).astype(BF), g2_ref[...])
    o_ref[0] = (y * gate).astype(o_ref.dtype)


def _rw_out(y0, y1, bon0, bon1, ul3, g2e, avg_bd, ln):
    tile = pl.BlockSpec((1, TOK_TILE, W), lambda b, i: (b, i, 0))
    return pl.pallas_call(
        _rw_out_kernel,
        grid=(NB, S // TOK_TILE),
        in_specs=[tile] * 4 + [
            tile,
            pl.BlockSpec((W, W), lambda b, i: (0, 0)),
            pl.BlockSpec((W, W), lambda b, i: (0, 0)),
            pl.BlockSpec((8, W), lambda b, i: (0, 0)),
        ],
        out_specs=tile,
        out_shape=jax.ShapeDtypeStruct((NB, S, W), BF),
        compiler_params=_params(("arbitrary", "arbitrary"), 32),
        name="rw_out",
    )(y0, y1, bon0, bon1, ul3, g2e, avg_bd, ln)


def _dft_mats():
    gd = W // 4
    cc = np.arange(gd)
    ang = 2 * np.pi * ((cc[:, None] * cc[None, :]) % gd) / gd
    chan = np.zeros((W, 2 * W), np.float32)
    for g in range(4):
        chan[g * gd:(g + 1) * gd, g * gd:(g + 1) * gd] = np.cos(ang)
        chan[g * gd:(g + 1) * gd, W + g * gd:W + (g + 1) * gd] = np.sin(ang)
    cos = np.zeros((S, S), np.float32)
    nsin = np.zeros((S, S), np.float32)
    scale = np.zeros((S, 1), np.float32)
    for lo, n in ((0, CTX), (CTX, T)):
        tt = np.arange(n)
        a = 2 * np.pi * ((tt[:, None] * tt[None, :]) % n) / n
        cos[lo:lo + n, lo:lo + n] = np.cos(a)
        nsin[lo:lo + n, lo:lo + n] = -np.sin(a)
        scale[lo:lo + n] = (n * gd) ** -0.5
    return chan, cos, nsin, scale


def _fn_tok_kernel(cos_ref, nsin_ref, z_ref, sc_ref, o_ref):
    z = z_ref[0]
    f = _dot(cos_ref[...], z[:, 0:W]) + _dot(nsin_ref[...], z[:, W:2 * W])
    o_ref[0] = (f * sc_ref[...]).astype(o_ref.dtype)


def _fourier(u2, chan, cos, nsin, scale):
    zcs = _matmul(u2, chan, tm=1024, tn=2 * W, out_dtype=BF, a_col_blk=U_FN_X, k=W, name="fn_chan")
    return pl.pallas_call(
        _fn_tok_kernel,
        grid=(NB, S // TOK_TILE),
        in_specs=[
            pl.BlockSpec((TOK_TILE, S), lambda b, i: (i, 0)),
            pl.BlockSpec((TOK_TILE, S), lambda b, i: (i, 0)),
            pl.BlockSpec((1, S, 2 * W), lambda b, i: (b, 0, 0)),
            pl.BlockSpec((TOK_TILE, 1), lambda b, i: (i, 0)),
        ],
        out_specs=pl.BlockSpec((1, TOK_TILE, W), lambda b, i: (b, i, 0)),
        out_shape=jax.ShapeDtypeStruct((NB, S, W), BF),
        compiler_params=_params(("arbitrary", "arbitrary"), 40),
        name="fn_tok",
    )(cos, nsin, zcs.reshape(NB, S, 2 * W), scale)


def _sc_kernel(b_ref, c_ref, x_ref, cp_ref, xp_ref, cn_ref, xn_ref, w_ref, o_ref):
    i = pl.program_id(1)
    n_t = S // TOK_TILE
    f32 = lambda ref: ref[0].astype(F32)
    z = f32(c_ref) * f32(x_ref)
    zp = _shift_prev(z, (f32(cp_ref) * f32(xp_ref))[HALO - 1:HALO, :], i <= 1, False)
    zn = _shift_prev(z, (f32(cn_ref) * f32(xn_ref))[0:1, :], (i == 0) | (i == n_t - 1), True)
    w = w_ref[...]
    o_ref[0] = (f32(b_ref) * (zp * w[0:1] + z * w[1:2] + zn * w[2:3])).astype(o_ref.dtype)


def _short_conv(u3, conv_w):
    hb = TOK_TILE // HALO

    def tile(blk):
        return pl.BlockSpec((1, TOK_TILE, W), lambda b, i: (b, i, blk))

    def prev(blk):
        return pl.BlockSpec((1, HALO, W), lambda b, i: (b, jnp.maximum(i * hb - 1, 0), blk))

    def nxt(blk):
        return pl.BlockSpec((1, HALO, W), lambda b, i: (b, jnp.minimum((i + 1) * hb, S // HALO - 1), blk))

    return pl.pallas_call(
        _sc_kernel,
        grid=(NB, S // TOK_TILE),
        in_specs=[tile(U_SC_B), tile(U_SC_C), tile(U_SC_X), prev(U_SC_C), prev(U_SC_X), nxt(U_SC_C), nxt(U_SC_X),
                  pl.BlockSpec((8, W), lambda b, i: (0, 0))],
        out_specs=pl.BlockSpec((1, TOK_TILE, W), lambda b, i: (b, i, 0)),
        out_shape=jax.ShapeDtypeStruct((NB, S, W), BF),
        compiler_params=_params(("arbitrary", "arbitrary"), 32),
        name="short_conv",
    )(*([u3] * 7), conv_w)


def _merge_kernel(y0_ref, y1_ref, y2_ref, y3_ref, g0_ref, g1_ref, g2_ref, g3_ref, w_ref, o_ref):
    acc = None
    for n, (y_ref, g_ref) in enumerate(((y0_ref, g0_ref), (y1_ref, g1_ref), (y2_ref, g2_ref), (y3_ref, g3_ref))):
        gate = 0.5 * jnp.tanh(0.5 * g_ref[...].astype(F32)) + 0.5
        t = gate * _dot(y_ref[...], w_ref[n])
        acc = t if acc is None else acc + t
    o_ref[...] = acc.astype(o_ref.dtype)


def _merge(branches, u2, w_branch):
    tm, tn = 512, 512

    def gate(n):
        return pl.BlockSpec((tm, tn), lambda i, j: (i, U_GATES + n * (D // tn) + j))

    return pl.pallas_call(
        _merge_kernel,
        grid=(M // tm, D // tn),
        in_specs=[pl.BlockSpec((tm, W), lambda i, j: (i, 0))] * 4 + [gate(n) for n in range(4)] + [
            pl.BlockSpec((4, W, tn), lambda i, j: (0, 0, j))],
        out_specs=pl.BlockSpec((tm, tn), lambda i, j: (i, j)),
        out_shape=jax.ShapeDtypeStruct((M, D), BF),
        compiler_params=_params(("arbitrary", "arbitrary"), 32),
        name="merge",
    )(*branches, u2, u2, u2, u2, w_branch)


def _ffn_kernel(te_ref, nv_ref, x_ref, w1_ref, w3_ref, w2_ref, o_ref):
    i, f = pl.program_id(0), pl.program_id(1)

    @pl.when(f == 0)
    def _():
        o_ref[...] = jnp.zeros_like(o_ref)

    @pl.when(i < nv_ref[0])
    def _():
        x = x_ref[...]
        a = _dot(x, w1_ref[0].astype(BF))
        b = _dot(x, w3_ref[0].astype(BF))
        g = (a * jax.nn.sigmoid(a) * b).astype(BF)
        o_ref[...] += _dot(g, w2_ref[0].astype(BF))


def _ffn(x, w1, w3, w2, tile_expert, n_valid, *, tm, tf):
    rows = x.shape[0]
    n_tiles = rows // tm

    n_f = D_FF // tf

    def tile(i, nv):
        return jnp.minimum(i, nv[0] - 1)

    def ff(i, f, nv):
        return jnp.where(i < nv[0], f, n_f - 1)

    once = dict(pipeline_mode=pl.Buffered(1))
    return pl.pallas_call(
        _ffn_kernel,
        grid_spec=pltpu.PrefetchScalarGridSpec(
            num_scalar_prefetch=2,
            grid=(n_tiles, n_f),
            in_specs=[
                pl.BlockSpec((tm, D), lambda i, f, te, nv: (tile(i, nv), 0)),
                pl.BlockSpec((1, D, tf), lambda i, f, te, nv: (te[tile(i, nv)], 0, ff(i, f, nv))),
                pl.BlockSpec((1, D, tf), lambda i, f, te, nv: (te[tile(i, nv)], 0, ff(i, f, nv))),
                pl.BlockSpec((1, tf, D), lambda i, f, te, nv: (te[tile(i, nv)], ff(i, f, nv), 0)),
            ],
            out_specs=pl.BlockSpec((tm, D), lambda i, f, te, nv: (i, 0), **once),
        ),
        out_shape=jax.ShapeDtypeStruct((rows, D), F32),
        compiler_params=_params(("arbitrary", "arbitrary"), 56),
        name="ffn",
    )(tile_expert, n_valid, x, w1, w3, w2)


def _gather_kernel(idx_ref, live_ref, src_ref, *rest, n_slots, n_rows, weighted):
    wt_ref = rest[0] if weighted else None
    o_ref, buf, sem = rest[-3:]
    g, n_live = pl.program_id(0), live_ref[0]

    def issue(step, slot):
        def body(k, carry):
            for s in range(n_slots):
                row = idx_ref[s * n_rows + step * GATHER_ROWS + k]
                pltpu.make_async_copy(src_ref.at[pl.ds(row, 1)], buf.at[slot, pl.ds(s * GATHER_ROWS + k, 1)],
                                      sem.at[slot]).start()
            return carry
        lax.fori_loop(0, GATHER_ROWS, body, 0, unroll=16)

    @pl.when(g == 0)
    def _():
        issue(0, 0)

    @pl.when(g + 1 < n_live)
    def _():
        issue(g + 1, (g + 1) % 2)

    @pl.when(g < n_live)
    def _():
        slot = g % 2
        pltpu.make_async_copy(src_ref.at[pl.ds(0, n_slots * GATHER_ROWS)], buf.at[slot], sem.at[slot]).wait()
        acc = None
        for s in range(n_slots):
            rows = buf[slot, s * GATHER_ROWS:(s + 1) * GATHER_ROWS]
            if wt_ref is not None:
                rows = rows * wt_ref[:, s:s + 1]
            acc = rows if acc is None else acc + rows
        o_ref[...] = acc.astype(o_ref.dtype)

    @pl.when(g >= n_live)
    def _():
        o_ref[...] = jnp.zeros_like(o_ref)


def _row_gather(src, idx, n_slots, out_dtype, n_live_rows=None, weights=None):
    n_rows = idx.shape[0] // n_slots
    n_steps = n_rows // GATHER_ROWS
    live = jnp.full((1,), n_steps, jnp.int32) if n_live_rows is None else n_live_rows // GATHER_ROWS
    weighted = weights is not None
    wt_spec = [pl.BlockSpec((GATHER_ROWS, n_slots), lambda g, idx, live: (g, 0))] if weighted else []
    return pl.pallas_call(
        functools.partial(_gather_kernel, n_slots=n_slots, n_rows=n_rows, weighted=weighted),
        grid_spec=pltpu.PrefetchScalarGridSpec(
            num_scalar_prefetch=2,
            grid=(n_steps,),
            in_specs=[pl.BlockSpec(memory_space=pl.ANY)] + wt_spec,
            out_specs=pl.BlockSpec((GATHER_ROWS, D), lambda g, idx, live: (g, 0)),
            scratch_shapes=[pltpu.VMEM((2, n_slots * GATHER_ROWS, D), F32), pltpu.SemaphoreType.DMA((2,))],
        ),
        out_shape=jax.ShapeDtypeStruct((n_rows, D), out_dtype),
        compiler_params=_params(("arbitrary",), 32),
        name="row_gather",
    )(idx, live, src, *([weights] if weighted else []))


def _moe(hf, logits, tok, w1, w3, w2, layer):
    n_tok = tok.shape[0]
    n_tiles = (2 * n_tok) // MOE_TM + N_EXP
    top_val, top_idx = lax.top_k(logits, 2)
    top_p = jax.nn.softmax(top_val, axis=-1)
    e_flat = top_idx.reshape(-1)
    onehot = (e_flat[:, None] == jnp.arange(N_EXP)[None, :]).astype(jnp.int32)
    rank = jnp.take_along_axis(jnp.cumsum(onehot, axis=0) - onehot, e_flat[:, None], axis=1)[:, 0]
    count = jnp.sum(onehot, axis=0)
    tiles_per = (count + MOE_TM - 1) // MOE_TM
    tile_end = jnp.cumsum(tiles_per)
    group_start = (tile_end - tiles_per) * MOE_TM
    pos = group_start[e_flat] + rank
    n_valid = tile_end[-1:].astype(jnp.int32)
    tile_expert = jnp.minimum(jnp.searchsorted(tile_end, jnp.arange(n_tiles), side="right"),
                              N_EXP - 1).astype(jnp.int32)
    rows = n_tiles * MOE_TM
    src = jnp.zeros((rows,), jnp.int32).at[pos].set(jnp.repeat(tok, 2))
    xs = _row_gather(hf, src, 1, BF, n_live_rows=n_valid * MOE_TM)
    ys = _ffn(xs, w1, w3, w2, tile_expert + layer * N_EXP, n_valid, tm=MOE_TM, tf=FF_TILE)
    return _row_gather(ys, pos.reshape(n_tok, 2).T.reshape(-1), 2, F32, weights=top_p)


def _low_rank_w_in(w, layer):
    return jnp.concatenate([w[layer, :, 2048:2304], w[layer, :, 3328:3456], jnp.zeros((D, 128), w.dtype)], axis=1)


def _rows_at(mat, off, n=W):
    return jnp.zeros((n, W), F32).at[off:off + mat.shape[0]].set(mat)


def _pad_rows(rows, n):
    z = jnp.stack(rows)
    return jnp.concatenate([z, jnp.zeros((n - z.shape[0],) + z.shape[1:], z.dtype)], axis=0)


def kernel(x, c, ctx, c_ctx, ada_w, ada_b, norm_mix, norm_ffn, w_in, na_rel_bias, rw_mu_kvr, rw_mu_lora, rw_w0, rw_w2, rw_a0, rw_a2, rw_k_k, rw_k_a, rw_r_k, rw_g2, rw_ln_w, rw_ln_b, sc_conv, w_branch, w_out, ffn_w1, ffn_w3, ffn_w2, moe_router, moe_w1, moe_w3, moe_w2, norm_final):
    xs = jnp.concatenate([ctx, x], axis=1)
    cvec = jnp.concatenate([c, c_ctx[None], jnp.zeros((3, D), F32)], axis=0)
    mods = _ada(cvec, ada_w, ada_b).reshape(DEPTH, 8, 6, D)

    def mod_table(gate, scale, shift):
        def pick(sel):
            if sel is None:
                return jnp.zeros((NB, 2, D), F32)
            v = mods[sel[0], :, sel[1]]
            return jnp.stack([jnp.broadcast_to(v[4], (NB, D)), v[:NB]], axis=1)
        t = jnp.stack([pick(gate), pick(scale), pick(shift)], axis=2)
        return jnp.concatenate([t, jnp.zeros((NB, 2, 5, D), F32)], axis=2)

    head_id = np.arange(W) // HD
    ones_bd = jnp.asarray((head_id[:, None] == head_id[None, :]).astype(np.float32), BF)
    chan, cos, nsin, fscale = _dft_mats()
    chan, cos, nsin, fscale = jnp.asarray(chan, BF), jnp.asarray(cos, BF), jnp.asarray(nsin, BF), jnp.asarray(fscale)

    delta = None
    for l in range(DEPTH):
        mod = mod_table(None if l == 0 else (l - 1, 5), (l, 1), (l, 0))
        if l == 0:
            (h,) = _resid_norm(xs, None, mod, norm_mix[l])
        else:
            xs, h = _resid_norm(xs, delta, mod, norm_mix[l])
        u2 = _w_in(h.reshape(M, D), w_in, l)
        u3 = u2.reshape(NB, S, U_COLS)
        ul3 = _matmul(h.reshape(M, D), _low_rank_w_in(w_in, l), tm=1024, tn=W, out_dtype=BF,
                      name="w_in_low_rank").reshape(NB, S, W)

        y_na = _attention(u3, _na_bias_table(na_rel_bias[l]))

        lora_mu = jnp.zeros((2, W), F32)
        lora_mu = lora_mu.at[:, LORA_W:LORA_W + 2 * LORA].set(
            jnp.broadcast_to(rw_mu_lora[l][:, 0].reshape(1, 2 * LORA), (2, 2 * LORA)))
        lora_mu = lora_mu.at[:, LORA_A:LORA_A + 2 * LORA].set(
            jnp.broadcast_to(rw_mu_lora[l][:, 1].reshape(1, 2 * LORA), (2, 2 * LORA)))
        vec = jnp.stack([_pad_rows([rw_mu_kvr[l][d, 0], rw_mu_kvr[l][d, 1], rw_mu_kvr[l][d, 2], lora_mu[d],
                                    rw_w0[l][d], rw_a0[l][d], rw_k_k[l], rw_k_a[l], rw_r_k[l]], 16)
                         for d in range(2)])
        w2e = jnp.stack([_rows_at(rw_w2[l][d], LORA * d, 2 * LORA) for d in range(2)])
        a2e = jnp.stack([_rows_at(rw_a2[l][d], LORA * d, 2 * LORA) for d in range(2)])
        seqs = [_rw_prep(u3, ul3, vec, w2e, a2e, ones_bd, d) for d in range(2)]
        ys = _rw_scan(seqs[0][:6], seqs[1][:6])
        bons = [seqs[0][6], seqs[1][6]]
        y_rw = _rw_out(ys[0], ys[1], bons[0], bons[1], ul3, _rows_at(rw_g2[l], LORA_G).astype(BF),
                       ones_bd / HD, _pad_rows([rw_ln_w[l], rw_ln_b[l]], 8))

        y_fn = _fourier(u2, chan, cos, nsin, fscale)
        y_sc = _short_conv(u3, _pad_rows([sc_conv[l][0], sc_conv[l][1], sc_conv[l][2]], 8))

        merged = _merge([z.reshape(M, W) for z in (y_na, y_rw, y_fn, y_sc)], u2, w_branch[l].astype(BF))
        merged = merged.reshape(NB, S, D)
        w_o = w_out[l].astype(BF)

        mod = mod_table((l, 2), (l, 4), (l, 3))
        i = l // 2
        last = l == DEPTH - 1
        if l % 2 == 0:
            xs, hf = _resid_norm(xs, merged, mod, norm_ffn[l], proj=w_o)
            delta = _ffn(hf.reshape(M, D), ffn_w1, ffn_w3, ffn_w2,
                         jnp.full((M // 1024,), i, jnp.int32), jnp.full((1,), M // 1024, jnp.int32),
                         tm=1024, tf=FF_TILE).reshape(NB, S, D)
            delta = delta[:, CTX:] if last else delta
        else:
            router = jnp.concatenate([moe_router[i], jnp.zeros((D, 128 - N_EXP), F32)], axis=1)
            xs, hf, logits = _resid_norm(xs, merged, mod, norm_ffn[l], router=router, proj=w_o)
            n_moe = moe_w1.shape[0]
            first = CTX if last else 0
            tok = jnp.arange(M, dtype=jnp.int32).reshape(NB, S)[:, first:].reshape(-1)
            delta = _moe(hf.reshape(M, D), logits[:, first:, :N_EXP].reshape(-1, N_EXP), tok,
                         moe_w1.reshape(n_moe * N_EXP, D, D_FF),
                         moe_w3.reshape(n_moe * N_EXP, D, D_FF), moe_w2.reshape(n_moe * N_EXP, D_FF, D), i)
            delta = delta.reshape(NB, -1, D)

    mod = mod_table((DEPTH - 1, 5), None, None)
    (out,) = _resid_norm(xs, delta, mod, norm_final, final=True)
    return out
```
